```python
import math
import jax, jax.numpy as jnp
from jax import lax
import numpy as np

D_MODEL = 1024
BATCH = 8
SEQ = 2048
DEPTH = 4
DEC_BATCH = 32
DEC_SEQ = 4
PAST_LEN = 8192
PAGE_SIZE = 128

N_MIXERS = 3
N_A_LAYERS = (DEPTH + 2) // 3
N_B_LAYERS = (DEPTH + 1) // 3
N_C_LAYERS = DEPTH // 3

A_HEADS = 8
A_DK = 128
A_DV = 128
A_QKV = A_HEADS * (2 * A_DK + A_DV)
A_IN = A_QKV + A_HEADS * A_DV + 2 * A_HEADS
A_CONV = 4
A_CHUNK = 64

B_CONV = 3

C_HEADS = 16
C_DH = D_MODEL // C_HEADS
C_QBLOCK = 128
C_LOGIT_OFFSET = 7.0

MEM_LEN = 256
M_HEADS = 4
M_DH = D_MODEL // M_HEADS

FF_DIM = 4 * D_MODEL
ALPHA = (2.0 * DEPTH) ** 0.25
BETA_INIT = (8.0 * DEPTH) ** -0.25
LN_EPS = 1e-5
NORM_EPS = 1e-6

kernel_name = 'hybrid_deltanet_shortconv_stickbreaking_step'


def layer_norm(x, g, b):
    xf = x.astype(jnp.float32)
    mu = jnp.mean(xf, -1, keepdims=True)
    var = jnp.mean(jnp.square(xf - mu), -1, keepdims=True)
    return ((xf - mu) * lax.rsqrt(var + LN_EPS) * g + b).astype(x.dtype)


def l2norm(x):
    return x * lax.rsqrt(jnp.sum(x * x, -1, keepdims=True) + NORM_EPS)


def causal_dwconv(x, buf, w):
    width = w.shape[0]
    t = x.shape[1]
    xp = jnp.concatenate([buf.astype(x.dtype), x], axis=1)
    y = sum(xp[:, i:i + t] * w[i] for i in range(width))
    return y, xp[:, t:]


def gated_delta_rule(q, k, v, g, beta, s0):
    b, t, h, dk = q.shape
    dv = v.shape[-1]
    L = A_CHUNK if t % A_CHUNK == 0 else t
    n = t // L

    def to_chunks(a):
        a = a.reshape((b, n, L) + a.shape[2:])
        return jnp.swapaxes(jnp.moveaxis(a, 1, 0), 2, 3)

    tri_incl = jnp.tril(jnp.ones((L, L), bool))
    strict = jnp.tril(jnp.ones((L, L), bool), k=-1)
    eye = jnp.eye(L, dtype=jnp.float32)

    def step(s, inp):
        qc, kc, vc, gc, bc = inp
        G = jnp.cumsum(gc, axis=-1)
        diff = G[..., :, None] - G[..., None, :]
        decay = jnp.exp(jnp.where(tri_incl, diff, -jnp.inf))
        kb = kc * bc[..., None]
        m = jnp.where(strict, jnp.einsum('bhik,bhjk->bhij', kb, kc) * decay, 0.0)
        rhs = jnp.concatenate([vc * bc[..., None], kb * jnp.exp(G)[..., None]], axis=-1)
        sol = lax.linalg.triangular_solve(m + eye, rhs, left_side=True, lower=True, unit_diagonal=True)
        u, w = sol[..., :dv], sol[..., dv:]
        v_new = u - jnp.einsum('bhlk,bhkv->bhlv', w, s)
        intra = jnp.einsum('bhik,bhjk->bhij', qc, kc) * decay
        o = (jnp.einsum('bhlk,bhkv->bhlv', qc * jnp.exp(G)[..., None], s)
             + jnp.einsum('bhij,bhjv->bhiv', intra, v_new))
        g_last = G[..., -1:]
        s_new = (s * jnp.exp(g_last)[..., None]
                 + jnp.einsum('bhlk,bhlv->bhkv', kc * jnp.exp(g_last - G)[..., None], v_new))
        return s_new, o

    s_fin, o = lax.scan(step, s0, (to_chunks(q), to_chunks(k), to_chunks(v), to_chunks(g), to_chunks(beta)))
    o = jnp.moveaxis(jnp.swapaxes(o, 2, 3), 0, 1).reshape(b, t, h, dv)
    return o, s_fin


def mixer_a(x, conv_buf, s0, w_in, conv_w, a_log, dt_bias, norm_w, w_out):
    b, t, _ = x.shape
    proj = x @ w_in
    qkv, z, a, bb = jnp.split(proj, [A_QKV, A_QKV + A_HEADS * A_DV, A_QKV + A_HEADS * A_DV + A_HEADS], axis=-1)
    qkv, new_buf = causal_dwconv(qkv, conv_buf, conv_w)
    qkv = jax.nn.silu(qkv).astype(jnp.float32)
    q, k, v = jnp.split(qkv, [A_HEADS * A_DK, 2 * A_HEADS * A_DK], axis=-1)
    q = l2norm(q.reshape(b, t, A_HEADS, A_DK)) * (A_DK ** -0.5)
    k = l2norm(k.reshape(b, t, A_HEADS, A_DK))
    v = v.reshape(b, t, A_HEADS, A_DV)
    g = -jnp.exp(a_log.astype(jnp.float32)) * jax.nn.softplus(a.astype(jnp.float32) + dt_bias.astype(jnp.float32))
    beta = jax.nn.sigmoid(bb.astype(jnp.float32))
    o, s = gated_delta_rule(q, k, v, g, beta, s0.astype(jnp.float32))
    z = z.reshape(b, t, A_HEADS, A_DV).astype(jnp.float32)
    o = o * lax.rsqrt(jnp.mean(o * o, -1, keepdims=True) + NORM_EPS) * norm_w * jax.nn.silu(z)
    y = o.reshape(b, t, A_HEADS * A_DV).astype(x.dtype) @ w_out
    return y, new_buf, s


def mixer_b(x, conv_buf, w_in, conv_w, w_out):
    gate_b, gate_c, h = jnp.split(x @ w_in, 3, axis=-1)
    c, new_buf = causal_dwconv(gate_c * h, conv_buf, conv_w)
    return (gate_b * c) @ w_out, new_buf


def stick_breaking(q, k, v, q_start):
    b, tq, h, dh = q.shape
    key_pos = jnp.arange(k.shape[1])

    def block(args):
        qb, qpos = args
        z = jnp.einsum('bqhd,bkhd->bhqk', qb, k).astype(jnp.float32) * (dh ** -0.5)
        mask = key_pos[None, :] < qpos[:, None]
        log_1mb = jnp.where(mask, jax.nn.log_sigmoid(-z), 0.0)
        suffix = lax.cumsum(log_1mb, axis=z.ndim - 1, reverse=True) - log_1mb
        w = jnp.where(mask, jnp.exp(jax.nn.log_sigmoid(z) + suffix), 0.0)
        return jnp.einsum('bhqk,bkhd->bqhd', w.astype(v.dtype), v)

    pos = q_start + jnp.arange(tq)
    if tq > C_QBLOCK and tq % C_QBLOCK == 0:
        nb = tq // C_QBLOCK
        qs = jnp.moveaxis(q.reshape(b, nb, C_QBLOCK, h, dh), 1, 0)
        out = lax.map(block, (qs, pos.reshape(nb, C_QBLOCK)))
        return jnp.moveaxis(out, 0, 1).reshape(b, tq, h, dh)
    return block((q, pos))


def mixer_c(x, k_past, v_past, w_qkv, b_qkv, w_out):
    b, t, _ = x.shape
    qkv = (x @ w_qkv).reshape(b, t, 3, C_HEADS, C_DH) + b_qkv
    q, k, v = qkv[:, :, 0], qkv[:, :, 1], qkv[:, :, 2]
    k_all = jnp.concatenate([k_past.astype(k.dtype), k], axis=1)
    v_all = jnp.concatenate([v_past.astype(v.dtype), v], axis=1)
    o = stick_breaking(q, k_all, v_all, k_past.shape[1])
    return o.reshape(b, t, D_MODEL) @ w_out, k, v


def mem_attend(x, mk, mv, w_q, w_o):
    b, t, _ = x.shape
    q = (x @ w_q).reshape(b, t, M_HEADS, M_DH)
    s = jnp.einsum('bqhd,bmhd->bhqm', q, mk.astype(x.dtype)).astype(jnp.float32) * (M_DH ** -0.5)
    p = jax.nn.softmax(s, axis=-1).astype(x.dtype)
    o = jnp.einsum('bhqm,bmhd->bqhd', p, mv.astype(x.dtype))
    return o.reshape(b, t, D_MODEL) @ w_o


def mlp(x, w1, w2):
    return jnp.square(jax.nn.relu(x @ w1)) @ w2


def trunk(x, a_conv, a_ssm, b_conv, c_kp, c_vp, mem_k, mem_v, params):
    (a_w_in, a_conv_w, a_a_log, a_dt_bias, a_norm_w, a_w_out, b_w_in, b_conv_w, b_w_out,
     c_w_qkv, c_b_qkv, c_w_out, m_w_q, m_w_o, f_w1, f_w2, ln_g, ln_b) = params
    na_conv, na_ssm, nb_conv, nc_k, nc_v = [], [], [], [], []
    for i in range(DEPTH):
        kind, j = i % N_MIXERS, i // N_MIXERS
        if kind == 0:
            y, cb, st = mixer_a(x, a_conv[j], a_ssm[j], a_w_in[j], a_conv_w[j], a_a_log[j],
                                a_dt_bias[j], a_norm_w[j], a_w_out[j])
            na_conv.append(cb)
            na_ssm.append(st)
        elif kind == 1:
            y, cb = mixer_b(x, b_conv[j], b_w_in[j], b_conv_w[j], b_w_out[j])
            nb_conv.append(cb)
        else:
            y, kn, vn = mixer_c(x, c_kp[j], c_vp[j], c_w_qkv[j], c_b_qkv[j], c_w_out[j])
            nc_k.append(kn)
            nc_v.append(vn)
        x = layer_norm(ALPHA * x + y, ln_g[i, 0], ln_b[i, 0])
        x = layer_norm(ALPHA * x + mem_attend(x, mem_k[i], mem_v[i], m_w_q[i], m_w_o[i]), ln_g[i, 1], ln_b[i, 1])
        x = layer_norm(ALPHA * x + mlp(x, f_w1[i], f_w2[i]), ln_g[i, 2], ln_b[i, 2])
    return (x, jnp.stack(na_conv), jnp.stack(na_ssm), jnp.stack(nb_conv), jnp.stack(nc_k), jnp.stack(nc_v))


def setup_inputs(seed: int = 0) -> dict:
    key = jax.random.key(seed)
    ks = jax.random.split(key, 40)
    f32 = jnp.float32
    n_pages = PAST_LEN // PAGE_SIZE
    n_phys = (DEC_BATCH * n_pages * 5) // 4
    nrm = lambda k, shape, s=1.0: jax.random.normal(k, shape, f32) * s
    dt = jnp.exp(jax.random.uniform(ks[20], (N_A_LAYERS, A_HEADS), f32, math.log(1e-3), math.log(1e-1)))
    page_table = jax.random.permutation(ks[10], n_phys)[:DEC_BATCH * n_pages].reshape(DEC_BATCH, n_pages).astype(jnp.int32)
    bias_scale = math.sqrt(C_LOGIT_OFFSET * math.sqrt(C_DH))
    u = l2norm(nrm(ks[29], (N_C_LAYERS, C_HEADS, C_DH)))
    c_b_qkv = jnp.stack([bias_scale * u,
                         -bias_scale * u + nrm(ks[30], (N_C_LAYERS, C_HEADS, C_DH), 0.02),
                         nrm(ks[31], (N_C_LAYERS, C_HEADS, C_DH), 0.02)], axis=1)
    cache_c_k = nrm(ks[5], (N_C_LAYERS, n_phys, PAGE_SIZE, C_HEADS, C_DH)) + c_b_qkv[:, 1][:, None, None]
    return {
        'x_prompt': nrm(ks[0], (BATCH, SEQ, D_MODEL)),
        'x_sample': nrm(ks[1], (DEC_BATCH, DEC_SEQ, D_MODEL)),
        'state_a_conv': nrm(ks[2], (N_A_LAYERS, DEC_BATCH, A_CONV - 1, A_QKV)),
        'state_a_ssm': nrm(ks[3], (N_A_LAYERS, DEC_BATCH, A_HEADS, A_DK, A_DV), 0.25),
        'state_b_conv': nrm(ks[4], (N_B_LAYERS, DEC_BATCH, B_CONV - 1, D_MODEL)),
        'cache_c_k': cache_c_k,
        'cache_c_v': nrm(ks[6], (N_C_LAYERS, n_phys, PAGE_SIZE, C_HEADS, C_DH)),
        'cache_mem_k': nrm(ks[7], (DEPTH, DEC_BATCH, MEM_LEN, M_HEADS, M_DH)),
        'cache_mem_v': nrm(ks[8], (DEPTH, DEC_BATCH, MEM_LEN, M_HEADS, M_DH)),
        'page_table': page_table,
        'mem_prompt': nrm(ks[9], (BATCH, MEM_LEN, D_MODEL)),
        'a_w_in': nrm(ks[11], (N_A_LAYERS, D_MODEL, A_IN), D_MODEL ** -0.5),
        'a_conv_w': nrm(ks[12], (N_A_LAYERS, A_CONV, A_QKV), A_CONV ** -0.5),
        'a_a_log': jnp.log(jax.random.uniform(ks[13], (N_A_LAYERS, A_HEADS), f32, 1.0, 16.0)),
        'a_dt_bias': dt + jnp.log(-jnp.expm1(-dt)),
        'a_norm_w': 1.0 + nrm(ks[14], (N_A_LAYERS, A_DV), 0.02),
        'a_w_out': nrm(ks[15], (N_A_LAYERS, A_HEADS * A_DV, D_MODEL), (A_HEADS * A_DV) ** -0.5 * BETA_INIT),
        'b_w_in': nrm(ks[16], (N_B_LAYERS, D_MODEL, 3 * D_MODEL), D_MODEL ** -0.5),
        'b_conv_w': nrm(ks[17], (N_B_LAYERS, B_CONV, D_MODEL), B_CONV ** -0.5),
        'b_w_out': nrm(ks[18], (N_B_LAYERS, D_MODEL, D_MODEL), D_MODEL ** -0.5 * BETA_INIT),
        'c_w_qkv': nrm(ks[19], (N_C_LAYERS, D_MODEL, 3 * D_MODEL), D_MODEL ** -0.5),
        'c_b_qkv': c_b_qkv,
        'c_w_out': nrm(ks[21], (N_C_LAYERS, D_MODEL, D_MODEL), D_MODEL ** -0.5 * BETA_INIT),
        'm_w_q': nrm(ks[22], (DEPTH, D_MODEL, D_MODEL), D_MODEL ** -0.5),
        'm_w_kv': nrm(ks[23], (DEPTH, D_MODEL, 2 * D_MODEL), D_MODEL ** -0.5),
        'm_w_o': nrm(ks[24], (DEPTH, D_MODEL, D_MODEL), D_MODEL ** -0.5 * BETA_INIT),
        'f_w1': nrm(ks[25], (DEPTH, D_MODEL, FF_DIM), D_MODEL ** -0.5),
        'f_w2': nrm(ks[26], (DEPTH, FF_DIM, D_MODEL), FF_DIM ** -0.5 * BETA_INIT),
        'ln_g': 1.0 + nrm(ks[27], (DEPTH, 3, D_MODEL), 0.02),
        'ln_b': nrm(ks[28], (DEPTH, 3, D_MODEL), 0.02),
    }


def reference(x_prompt, x_sample, state_a_conv, state_a_ssm, state_b_conv, cache_c_k, cache_c_v,
              cache_mem_k, cache_mem_v, page_table, mem_prompt, a_w_in, a_conv_w, a_a_log, a_dt_bias,
              a_norm_w, a_w_out, b_w_in, b_conv_w, b_w_out, c_w_qkv, c_b_qkv, c_w_out, m_w_q, m_w_kv, m_w_o,
              f_w1, f_w2, ln_g, ln_b):
    params = (a_w_in, a_conv_w, a_a_log, a_dt_bias, a_norm_w, a_w_out, b_w_in, b_conv_w, b_w_out,
              c_w_qkv, c_b_qkv, c_w_out, m_w_q, m_w_o, f_w1, f_w2, ln_g, ln_b)
    bp = x_prompt.shape[0]
    dt = x_prompt.dtype
    n_mem = mem_prompt.shape[1]
    mkv = jnp.einsum('bmd,ldn->lbmn', mem_prompt, m_w_kv)
    p_mem_k = mkv[..., :D_MODEL].reshape(DEPTH, bp, n_mem, M_HEADS, M_DH)
    p_mem_v = mkv[..., D_MODEL:].reshape(DEPTH, bp, n_mem, M_HEADS, M_DH)
    y_prompt, p_a_conv, p_a_ssm, p_b_conv, p_c_k, p_c_v = trunk(
        x_prompt,
        jnp.zeros((N_A_LAYERS, bp, A_CONV - 1, A_QKV), dt),
        jnp.zeros((N_A_LAYERS, bp, A_HEADS, A_DK, A_DV), jnp.float32),
        jnp.zeros((N_B_LAYERS, bp, B_CONV - 1, D_MODEL), dt),
        jnp.zeros((N_C_LAYERS, bp, 0, C_HEADS, C_DH), dt),
        jnp.zeros((N_C_LAYERS, bp, 0, C_HEADS, C_DH), dt),
        p_mem_k, p_mem_v, params)
    n_c, _, page, h, dh = cache_c_k.shape
    db, n_pages = page_table.shape
    past_k = cache_c_k[:, page_table].reshape(n_c, db, n_pages * page, h, dh)
    past_v = cache_c_v[:, page_table].reshape(n_c, db, n_pages * page, h, dh)
    y_sample, s_a_conv, s_a_ssm, s_b_conv, s_c_k, s_c_v = trunk(
        x_sample, state_a_conv, state_a_ssm, state_b_conv, past_k, past_v,
        cache_mem_k, cache_mem_v, params)
    return (y_prompt, y_sample, p_a_conv, p_a_ssm, p_b_conv, p_c_k, p_c_v, p_mem_k, p_mem_v,
            s_a_conv, s_a_ssm, s_b_conv, s_c_k, s_c_v)
```

```python
import functools
import math

import jax
import jax.numpy as jnp
from jax import lax
from jax.experimental import pallas as pl
from jax.experimental.pallas import tpu as pltpu

F32 = jnp.float32
BF16 = jnp.bfloat16
HIGHEST = lax.Precision.HIGHEST

V7X_LANES = 128
V7X_SUBLANES = 8
V7X_VMEM_BYTES = 64 * 1024 * 1024
VMEM_CAP_BYTES = V7X_VMEM_BYTES - 8 * 1024 * 1024

LN_EPS = 1e-5
NORM_EPS = 1e-6
A_CHUNK = 64
A_HEADS = 8
A_DK = 128
C_HEADS = 16
M_HEADS = 4
SAMPLE_T_PAD = V7X_SUBLANES
PROMPT_TILE = 256


def _vmem_limit(block_bytes, scratch_bytes=0, temp_bytes=0):
    est = 2 * block_bytes + scratch_bytes + temp_bytes + (4 << 20)
    return int(min(max(est, 16 << 20), VMEM_CAP_BYTES))


def _nbytes(shape, dtype):
    return math.prod(shape) * jnp.dtype(dtype).itemsize


def _params(sem, block_bytes, scratch_bytes=0, temp_bytes=0):
    return pltpu.CompilerParams(dimension_semantics=sem,
                                vmem_limit_bytes=_vmem_limit(block_bytes, scratch_bytes, temp_bytes))


def _layer_norm(v, g, b):
    mu = jnp.mean(v, -1, keepdims=True)
    c = v - mu
    var = jnp.mean(c * c, -1, keepdims=True)
    return c * lax.rsqrt(var + LN_EPS) * g + b


def _softplus(x):
    return jnp.maximum(x, 0.0) + jnp.log1p(jnp.exp(-jnp.abs(x)))


def _silu(x):
    return x * jax.nn.sigmoid(x)


def _dot(a, b, **kw):
    return jnp.dot(a, b, preferred_element_type=F32, **kw)


def _dot_nt(a, b):
    return lax.dot_general(a, b, (((1,), (1,)), ((), ())), preferred_element_type=F32)


def _mm_split_kernel(x_ref, w_ref, b_ref, *o_refs, d, scales, n_copies):
    x = x_ref[...]
    k = 0
    for j, n_out in enumerate(n_copies):
        cols = slice(j * d, (j + 1) * d)
        y = _dot(x, w_ref[:, cols]) + b_ref[:, cols]
        if scales[j] != 1.0:
            y = y * scales[j]
        for _ in range(n_out):
            o_refs[k][...] = y.astype(o_refs[k].dtype)
            k += 1


def mm_split(x, w, bias, out_dtypes, scales, tm):
    m, kdim = x.shape
    ns = len(out_dtypes)
    d = w.shape[1] // ns
    flat = [dt for group in out_dtypes for dt in group]
    blocks = (_nbytes((tm, kdim), x.dtype) + _nbytes(w.shape, w.dtype) + _nbytes(bias.shape, F32)
              + sum(_nbytes((tm, d), dt) for dt in flat))
    return pl.pallas_call(
        functools.partial(_mm_split_kernel, d=d, scales=tuple(scales), n_copies=tuple(len(g) for g in out_dtypes)),
        grid=(m // tm,),
        in_specs=[pl.BlockSpec((tm, kdim), lambda i: (i, 0)),
                  pl.BlockSpec(w.shape, lambda i: (0, 0)),
                  pl.BlockSpec(bias.shape, lambda i: (0, 0))],
        out_specs=[pl.BlockSpec((tm, d), lambda i: (i, 0)) for _ in flat],
        out_shape=[jax.ShapeDtypeStruct((m, d), dt) for dt in flat],
        compiler_params=_params(("parallel",), blocks, temp_bytes=2 * _nbytes((tm, d), F32)),
        name="mm_split",
    )(x, w, bias)


def _mm_ln_kernel(x_ref, w_ref, res_ref, g_ref, b_ref, of_ref, ob_ref, *, alpha):
    y = _dot(x_ref[...], w_ref[...])
    v = _layer_norm(alpha * res_ref[...] + y, g_ref[...], b_ref[...])
    of_ref[...] = v
    ob_ref[...] = v.astype(BF16)


def mm_ln(x, w, res, g, b, alpha, tm):
    m, kdim = x.shape
    d = w.shape[1]
    blocks = (_nbytes((tm, kdim), BF16) + _nbytes(w.shape, BF16) + 2 * _nbytes((tm, d), F32)
              + _nbytes((tm, d), BF16))
    row = lambda i: (i, 0)
    const = lambda i: (0, 0)
    return pl.pallas_call(
        functools.partial(_mm_ln_kernel, alpha=alpha),
        grid=(m // tm,),
        in_specs=[pl.BlockSpec((tm, kdim), row), pl.BlockSpec(w.shape, const), pl.BlockSpec((tm, d), row),
                  pl.BlockSpec((1, d), const), pl.BlockSpec((1, d), const)],
        out_specs=[pl.BlockSpec((tm, d), row), pl.BlockSpec((tm, d), row)],
        out_shape=[jax.ShapeDtypeStruct((m, d), F32), jax.ShapeDtypeStruct((m, d), BF16)],
        compiler_params=_params(("parallel",), blocks, temp_bytes=3 * _nbytes((tm, d), F32)),
        name="mm_ln",
    )(x, w, res, g, b)


def _mlp_kernel(x_ref, w1_ref, w2_ref, res_ref, g_ref, b_ref, of_ref, ob_ref, *, alpha, tf):
    x = x_ref[...]
    acc = None
    for c in range(w1_ref.shape[1] // tf):
        cols = slice(c * tf, (c + 1) * tf)
        h = jnp.square(jnp.maximum(_dot(x, w1_ref[:, cols]), 0.0)).astype(BF16)
        part = _dot(h, w2_ref[cols, :])
        acc = part if acc is None else acc + part
    v = _layer_norm(alpha * res_ref[...] + acc, g_ref[...], b_ref[...])
    of_ref[...] = v
    ob_ref[...] = v.astype(BF16)


def mlp_ln(x, w1, w2, res, g, b, alpha, tm, tf=1024):
    m, d = x.shape
    blocks = (_nbytes((tm, d), BF16) + _nbytes(w1.shape, BF16) + _nbytes(w2.shape, BF16)
              + 2 * _nbytes((tm, d), F32) + _nbytes((tm, d), BF16))
    row = lambda i: (i, 0)
    const = lambda i: (0, 0)
    return pl.pallas_call(
        functools.partial(_mlp_kernel, alpha=alpha, tf=tf),
        grid=(m // tm,),
        in_specs=[pl.BlockSpec((tm, d), row), pl.BlockSpec(w1.shape, const), pl.BlockSpec(w2.shape, const),
                  pl.BlockSpec((tm, d), row), pl.BlockSpec((1, d), const), pl.BlockSpec((1, d), const)],
        out_specs=[pl.BlockSpec((tm, d), row), pl.BlockSpec((tm, d), row)],
        out_shape=[jax.ShapeDtypeStruct((m, d), F32), jax.ShapeDtypeStruct((m, d), BF16)],
        compiler_params=_params(("parallel",), blocks, temp_bytes=3 * _nbytes((tm, tf), F32) + 2 * _nbytes((tm, d), F32)),
        name="mlp_ln",
    )(x, w1, w2, res, g, b)


def _memattn_kernel(x_ref, res_ref, mk_ref, mv_ref, wq_ref, wo_ref, g_ref, b_ref, of_ref, ob_ref, *, alpha, n_heads):
    d = x_ref.shape[1]
    dh = d // n_heads
    q = (_dot(x_ref[...], wq_ref[...]) * (dh ** -0.5)).astype(BF16)
    heads = []
    for h in range(n_heads):
        cols = slice(h * dh, (h + 1) * dh)
        s = _dot_nt(q[:, cols], mk_ref[:, cols])
        e = jnp.exp(s - jnp.max(s, -1, keepdims=True))
        p = e / jnp.sum(e, -1, keepdims=True)
        heads.append(_dot(p.astype(BF16), mv_ref[:, cols]).astype(BF16))
    o = jnp.concatenate(heads, axis=1)
    v = _layer_norm(alpha * res_ref[...] + _dot(o, wo_ref[...]), g_ref[...], b_ref[...])
    of_ref[...] = v
    ob_ref[...] = v.astype(BF16)


def memattn_prompt(xb, xf, mk, mv, wq, wo, g, b, alpha, n_seq, tm):
    m, d = xb.shape
    n_t = m // n_seq // tm
    n_mem = mk.shape[0] // n_seq
    blocks = (_nbytes((tm, d), BF16) * 2 + _nbytes((tm, d), F32) * 2 + 2 * _nbytes((n_mem, d), BF16)
              + 2 * _nbytes(wq.shape, BF16))
    row = lambda s, t: (s * n_t + t, 0)
    seq = lambda s, t: (s, 0)
    const = lambda s, t: (0, 0)
    return pl.pallas_call(
        functools.partial(_memattn_kernel, alpha=alpha, n_heads=M_HEADS),
        grid=(n_seq, n_t),
        in_specs=[pl.BlockSpec((tm, d), row), pl.BlockSpec((tm, d), row),
                  pl.BlockSpec((n_mem, d), seq), pl.BlockSpec((n_mem, d), seq),
                  pl.BlockSpec(wq.shape, const), pl.BlockSpec(wo.shape, const),
                  pl.BlockSpec((1, d), const), pl.BlockSpec((1, d), const)],
        out_specs=[pl.BlockSpec((tm, d), row), pl.BlockSpec((tm, d), row)],
        out_shape=[jax.ShapeDtypeStruct((m, d), F32), jax.ShapeDtypeStruct((m, d), BF16)],
        compiler_params=_params(("parallel", "parallel"), blocks, temp_bytes=6 * _nbytes((tm, d), F32)),
        name="memattn_prompt",
    )(xb, xf, mk, mv, wq, wo, g, b)


def _head_lane_mask(n_heads, d):
    lane_head = lax.broadcasted_iota(jnp.int32, (1, d), 1) // (d // n_heads)
    return [lane_head == h for h in range(n_heads)]


def _memattn_sample_kernel(q_ref, mk_ref, mv_ref, o_ref, *, n_heads):
    q = q_ref[0]
    d = q.shape[1]
    masks = _head_lane_mask(n_heads, d)
    qexp = jnp.concatenate([jnp.where(mk_h, q, 0.0) for mk_h in masks], axis=0).astype(BF16)
    s = _dot_nt(qexp, mk_ref[0].astype(BF16))
    e = jnp.exp(s - jnp.max(s, -1, keepdims=True))
    p = e / jnp.sum(e, -1, keepdims=True)
    o_all = _dot(p.astype(BF16), mv_ref[0].astype(BF16))
    tp = q.shape[0]
    out = jnp.zeros_like(q)
    for h in range(n_heads):
        out = out + jnp.where(masks[h], o_all[h * tp:(h + 1) * tp, :], 0.0)
    o_ref[0] = out.astype(o_ref.dtype)


def memattn_sample_core(q, mk, mv):
    n_seq, tp, d = q.shape
    n_mem = mk.shape[1]
    blocks = 2 * _nbytes((tp, d), F32) + 2 * _nbytes((n_mem, d), F32)
    return pl.pallas_call(
        functools.partial(_memattn_sample_kernel, n_heads=M_HEADS),
        grid=(n_seq,),
        in_specs=[pl.BlockSpec((1, tp, d), lambda s: (s, 0, 0)),
                  pl.BlockSpec((1, n_mem, d), lambda s: (s, 0, 0)),
                  pl.BlockSpec((1, n_mem, d), lambda s: (s, 0, 0))],
        out_specs=pl.BlockSpec((1, tp, d), lambda s: (s, 0, 0)),
        out_shape=jax.ShapeDtypeStruct((n_seq, tp, d), F32),
        compiler_params=_params(("parallel",), blocks, temp_bytes=2 * _nbytes((n_mem, d), F32)),
        name="memattn_sample",
    )(q, mk, mv)


A_HALO_ROW = V7X_SUBLANES


def _a_prep_kernel(x_ref, wqkvz_ref, wab_ref, cw_ref, alog_ref, dtb_ref, cs_ref,
                   q_ref, k_ref, v_ref, z_ref, gb_ref, cso_ref, xp_ref, *, tm, n_t, t_valid_last, n_heads):
    t = pl.program_id(1)
    width = cw_ref.shape[0]
    halo = width - 1
    r0 = A_HALO_ROW
    d = q_ref.shape[1]
    dk = d // n_heads

    @pl.when(t == 0)
    def _():
        xp_ref[r0 - halo:r0, :] = cs_ref[0]

    @pl.when(t > 0)
    def _():
        xp_ref[r0 - halo:r0, :] = xp_ref[r0 + tm - halo:r0 + tm, :]

    x = x_ref[...].astype(BF16)
    for seg, o_ref in enumerate((q_ref, k_ref, v_ref)):
        cols = slice(seg * d, (seg + 1) * d)
        xp_ref[r0:r0 + tm, cols] = _dot(x, wqkvz_ref[:, cols])
        y = xp_ref[r0 - halo:r0 - halo + tm, cols] * cw_ref[0:1, cols]
        for i in range(1, width):
            y = y + xp_ref[r0 - halo + i:r0 - halo + i + tm, cols] * cw_ref[i:i + 1, cols]
        y = _silu(y)
        if seg < 2:
            parts = []
            for h in range(n_heads):
                yh = y[:, h * dk:(h + 1) * dk]
                yh = yh * lax.rsqrt(jnp.sum(yh * yh, -1, keepdims=True) + NORM_EPS)
                parts.append(yh * (dk ** -0.5) if seg == 0 else yh)
            y = jnp.concatenate(parts, axis=1)
        o_ref[...] = y
    z_ref[...] = _dot(x, wqkvz_ref[:, 3 * d:4 * d])
    ab = _dot(x, wab_ref[...])
    g = -jnp.exp(alog_ref[...]) * _softplus(ab + dtb_ref[...])
    lane = lax.broadcasted_iota(jnp.int32, ab.shape, 1)
    gb_ref[...] = jnp.where(lane < n_heads, g, jax.nn.sigmoid(ab))

    @pl.when(t == n_t - 1)
    def _():
        cso_ref[0] = xp_ref[r0 + t_valid_last - halo:r0 + t_valid_last, :]


def a_prep(xf, wqkvz, wab, conv_w, alog, dtb, conv_state, n_seq, tm, t_valid_last):
    m, d = xf.shape
    n_t = m // n_seq // tm
    halo = conv_w.shape[0] - 1
    cq = conv_w.shape[1]
    row = lambda s, t: (s * n_t + t, 0)
    const = lambda s, t: (0, 0)
    seq3 = lambda s, t: (s, 0, 0)
    blocks = (_nbytes((tm, d), F32) + _nbytes(wqkvz.shape, BF16) + _nbytes(wab.shape, BF16)
              + 4 * _nbytes((tm, d), F32) + _nbytes((tm, V7X_LANES), F32) + 2 * _nbytes((V7X_SUBLANES, cq), F32))
    scratch = _nbytes((A_HALO_ROW + tm, cq), F32)
    outs = pl.pallas_call(
        functools.partial(_a_prep_kernel, tm=tm, n_t=n_t, t_valid_last=t_valid_last, n_heads=A_HEADS),
        grid=(n_seq, n_t),
        in_specs=[pl.BlockSpec((tm, d), row), pl.BlockSpec(wqkvz.shape, const), pl.BlockSpec(wab.shape, const),
                  pl.BlockSpec(conv_w.shape, const), pl.BlockSpec(alog.shape, const), pl.BlockSpec(dtb.shape, const),
                  pl.BlockSpec((1, halo, cq), seq3)],
        out_specs=[pl.BlockSpec((tm, d), row)] * 4 + [pl.BlockSpec((tm, V7X_LANES), row),
                                                      pl.BlockSpec((1, halo, cq), seq3)],
        out_shape=[jax.ShapeDtypeStruct((m, d), F32)] * 4 + [jax.ShapeDtypeStruct((m, V7X_LANES), F32),
                                                             jax.ShapeDtypeStruct((n_seq, halo, cq), F32)],
        scratch_shapes=[pltpu.VMEM((A_HALO_ROW + tm, cq), F32)],
        compiler_params=_params(("parallel", "arbitrary"), blocks, scratch, temp_bytes=6 * _nbytes((tm, d), F32)),
        name="a_prep",
    )(xf, wqkvz, wab, conv_w, alog, dtb, conv_state)
    return outs


def _gdn_kernel(q_ref, k_ref, v_ref, z_ref, gb_ref, s0_ref, nw_ref, o_ref, s_ref, *, t_valid, n_heads):
    c = pl.program_id(1)
    ln = q_ref.shape[0]
    dk = q_ref.shape[1] // n_heads

    @pl.when(c == 0)
    def _():
        s_ref[...] = s0_ref[...]

    row = lax.broadcasted_iota(jnp.int32, (ln, ln), 0)
    col = lax.broadcasted_iota(jnp.int32, (ln, ln), 1)
    tri_incl = row >= col
    strict = row > col
    eye = (row == col).astype(F32)
    gb = gb_ref[...]
    row_valid = None
    if t_valid < ln:
        row_valid = lax.broadcasted_iota(jnp.int32, (ln, 1), 0) < t_valid
        gb = jnp.where(row_valid, gb, 0.0)
    g_cum = _dot(tri_incl.astype(F32), gb, precision=HIGHEST)
    g_cum_t = g_cum.T
    e_cum = jnp.exp(g_cum)
    g_last = g_cum[ln - 1:ln, :]
    e_last = jnp.exp(g_last)
    e_rem = jnp.exp(g_last - g_cum)
    nw = nw_ref[...]
    for h in range(n_heads):
        cols = slice(h * dk, (h + 1) * dk)
        beta = gb[:, n_heads + h:n_heads + h + 1]
        decay = jnp.where(tri_incl, jnp.exp(g_cum[:, h:h + 1] - g_cum_t[h:h + 1, :]), 0.0)
        qh = q_ref[:, cols]
        kh = k_ref[:, cols]
        vh = v_ref[:, cols]
        if row_valid is not None:
            kh = jnp.where(row_valid, kh, 0.0)
        kh_b = kh.astype(BF16)
        kb = kh * beta
        m_mat = jnp.where(strict, _dot_nt(kb.astype(BF16), kh_b) * decay, 0.0)
        rhs = jnp.concatenate([vh * beta, kb * e_cum[:, h:h + 1]], axis=1)
        neg = -m_mat
        inv = eye + neg
        pw = neg
        for _ in range(int(math.log2(ln)) - 1):
            pw = _dot(pw, pw, precision=HIGHEST)
            inv = inv + _dot(inv, pw, precision=HIGHEST)
        sol = _dot(inv, rhs, precision=HIGHEST)
        u = sol[:, :dk]
        w = sol[:, dk:]
        s_h = s_ref[0, h]
        s_b = s_h.astype(BF16)
        v_new = u - _dot(w.astype(BF16), s_b)
        v_new_b = v_new.astype(BF16)
        intra = _dot_nt(qh.astype(BF16), kh_b) * decay
        o = _dot((qh * e_cum[:, h:h + 1]).astype(BF16), s_b) + _dot(intra.astype(BF16), v_new_b)
        kd_t = (kh * e_rem[:, h:h + 1]).T.astype(BF16)
        s_ref[0, h] = s_h * e_last[:, h:h + 1] + _dot(kd_t, v_new_b)
        zh = z_ref[:, cols]
        o = o * lax.rsqrt(jnp.mean(o * o, -1, keepdims=True) + NORM_EPS) * nw * _silu(zh)
        o_ref[:, cols] = o.astype(o_ref.dtype)


def gdn(q, k, v, z, gb, s0, norm_w, n_seq, t_valid):
    m, d = q.shape
    ln = A_CHUNK
    n_c = m // n_seq // ln
    row = lambda s, c: (s * n_c + c, 0)
    const = lambda s, c: (0, 0)
    seq4 = lambda s, c: (s, 0, 0, 0)
    sblock = (1,) + s0.shape[1:]
    blocks = (4 * _nbytes((ln, d), F32) + _nbytes((ln, V7X_LANES), F32) + 2 * _nbytes(sblock, F32)
              + _nbytes((ln, d), BF16))
    return pl.pallas_call(
        functools.partial(_gdn_kernel, t_valid=t_valid, n_heads=A_HEADS),
        grid=(n_seq, n_c),
        in_specs=[pl.BlockSpec((ln, d), row)] * 4 + [pl.BlockSpec((ln, V7X_LANES), row),
                                                     pl.BlockSpec(sblock, seq4), pl.BlockSpec(norm_w.shape, const)],
        out_specs=[pl.BlockSpec((ln, d), row), pl.BlockSpec(sblock, seq4)],
        out_shape=[jax.ShapeDtypeStruct((m, d), BF16), jax.ShapeDtypeStruct(s0.shape, F32)],
        compiler_params=_params(("parallel", "arbitrary"), blocks, temp_bytes=8 << 20),
        name="gdn",
    )(q, k, v, z, gb, s0, norm_w)


def _b_mixer_kernel(res_ref, win_ref, cw_ref, wout_ref, cs_ref, g_ref, b_ref,
                    of_ref, ob_ref, cso_ref, xp_ref, *, alpha, tm, n_t, t_valid_last):
    t = pl.program_id(1)
    width = cw_ref.shape[0]
    halo = width - 1
    r0 = A_HALO_ROW
    d = res_ref.shape[1]

    @pl.when(t == 0)
    def _():
        xp_ref[r0 - halo:r0, :] = cs_ref[0]

    @pl.when(t > 0)
    def _():
        xp_ref[r0 - halo:r0, :] = xp_ref[r0 + tm - halo:r0 + tm, :]

    x = res_ref[...].astype(BF16)
    gate_c = _dot(x, win_ref[:, d:2 * d])
    hid = _dot(x, win_ref[:, 2 * d:3 * d])
    xp_ref[r0:r0 + tm, :] = gate_c * hid
    y = xp_ref[r0 - halo:r0 - halo + tm, :] * cw_ref[0:1, :]
    for i in range(1, width):
        y = y + xp_ref[r0 - halo + i:r0 - halo + i + tm, :] * cw_ref[i:i + 1, :]
    gate_b = _dot(x, win_ref[:, 0:d])
    out = _dot((gate_b * y).astype(BF16), wout_ref[...])
    v = _layer_norm(alpha * res_ref[...] + out, g_ref[...], b_ref[...])
    of_ref[...] = v
    ob_ref[...] = v.astype(BF16)

    @pl.when(t == n_t - 1)
    def _():
        cso_ref[0] = xp_ref[r0 + t_valid_last - halo:r0 + t_valid_last, :]


def b_mixer(xf, w_in, conv_w, w_out, conv_state, g, b, alpha, n_seq, tm, t_valid_last):
    m, d = xf.shape
    n_t = m // n_seq // tm
    halo = conv_w.shape[0] - 1
    row = lambda s, t: (s * n_t + t, 0)
    const = lambda s, t: (0, 0)
    seq3 = lambda s, t: (s, 0, 0)
    blocks = (_nbytes((tm, d), BF16) + 2 * _nbytes((tm, d), F32) + _nbytes(w_in.shape, BF16)
              + _nbytes(w_out.shape, BF16) + 2 * _nbytes((V7X_SUBLANES, d), F32))
    scratch = _nbytes((A_HALO_ROW + tm, d), F32)
    return pl.pallas_call(
        functools.partial(_b_mixer_kernel, alpha=alpha, tm=tm, n_t=n_t, t_valid_last=t_valid_last),
        grid=(n_seq, n_t),
        in_specs=[pl.BlockSpec((tm, d), row), pl.BlockSpec(w_in.shape, const),
                  pl.BlockSpec(conv_w.shape, const), pl.BlockSpec(w_out.shape, const),
                  pl.BlockSpec((1, halo, d), seq3), pl.BlockSpec((1, d), const), pl.BlockSpec((1, d), const)],
        out_specs=[pl.BlockSpec((tm, d), row), pl.BlockSpec((tm, d), row), pl.BlockSpec((1, halo, d), seq3)],
        out_shape=[jax.ShapeDtypeStruct((m, d), F32), jax.ShapeDtypeStruct((m, d), BF16),
                   jax.ShapeDtypeStruct((n_seq, halo, d), F32)],
        scratch_shapes=[pltpu.VMEM((A_HALO_ROW + tm, d), F32)],
        compiler_params=_params(("parallel", "arbitrary"), blocks, scratch, temp_bytes=6 * _nbytes((tm, d), F32)),
        name="b_mixer",
    )(xf, w_in, conv_w, w_out, conv_state, g, b)


def _suffix_matrix(n):
    r = lax.broadcasted_iota(jnp.int32, (n, n), 0)
    c = lax.broadcasted_iota(jnp.int32, (n, n), 1)
    return (r > c).astype(BF16)


def _suffix_sum(l, u):
    l_hi = l.astype(BF16)
    l_lo = (l - l_hi.astype(F32)).astype(BF16)
    return _dot(l_hi, u) + _dot(l_lo, u)


def _sb_prompt_kernel(q_ref, k_ref, v_ref, u_ref, o_ref, *, tq, dh):
    i = pl.program_id(2)
    lanes = q_ref.shape[1]
    lane = lax.broadcasted_iota(jnp.int32, (1, lanes), 1)
    n_sub = lanes // dh
    q = q_ref[...]
    qs = [jnp.where(lane // dh == hh, q, jnp.zeros_like(q)) for hh in range(n_sub)]
    r = lax.broadcasted_iota(jnp.int32, (tq, tq), 0)
    c = lax.broadcasted_iota(jnp.int32, (tq, tq), 1)
    causal = c < r
    u = u_ref[...]

    def visit(j, state, diagonal):
        accs, carries = state
        start = pl.multiple_of(j * tq, tq)
        kj = k_ref[pl.ds(start, tq), :]
        vj = v_ref[pl.ds(start, tq), :]
        new_accs, new_carries = [], []
        for hh in range(n_sub):
            s = _dot_nt(qs[hh], kj)
            sp = _softplus(s)
            l = -sp
            if diagonal:
                l = jnp.where(causal, l, 0.0)
            w = jnp.exp(s - sp + _suffix_sum(l, u) + carries[hh])
            if diagonal:
                w = jnp.where(causal, w, 0.0)
            new_accs.append(accs[hh] + _dot(w.astype(BF16), vj))
            new_carries.append(carries[hh] + jnp.sum(l, -1, keepdims=True))
        return tuple(new_accs), tuple(new_carries)

    init = (tuple(jnp.zeros((tq, lanes), F32) for _ in range(n_sub)),
            tuple(jnp.zeros((tq, 1), F32) for _ in range(n_sub)))
    state = visit(i, init, True)
    state = lax.fori_loop(0, i, lambda t, st: visit(i - 1 - t, st, False), state)
    accs = state[0]
    out = accs[0]
    for hh in range(1, n_sub):
        out = jnp.where(lane // dh == hh, accs[hh], out)
    o_ref[...] = out.astype(o_ref.dtype)


def sb_prompt(q, k, v, n_seq, tq):
    m, d = q.shape
    t_len = m // n_seq
    n_q = t_len // tq
    dh = d // C_HEADS
    lanes = V7X_LANES
    u = jnp.tril(jnp.ones((tq, tq), BF16), k=-1)
    blocks = 2 * _nbytes((tq, lanes), BF16) + 2 * _nbytes((t_len, lanes), BF16) + _nbytes((tq, tq), BF16)
    return pl.pallas_call(
        functools.partial(_sb_prompt_kernel, tq=tq, dh=dh),
        grid=(n_seq, d // lanes, n_q),
        in_specs=[pl.BlockSpec((tq, lanes), lambda s, p, i: (s * n_q + i, p)),
                  pl.BlockSpec((t_len, lanes), lambda s, p, i: (s, p)),
                  pl.BlockSpec((t_len, lanes), lambda s, p, i: (s, p)),
                  pl.BlockSpec((tq, tq), lambda s, p, i: (0, 0))],
        out_specs=pl.BlockSpec((tq, lanes), lambda s, p, i: (s * n_q + i, p)),
        out_shape=jax.ShapeDtypeStruct((m, d), BF16),
        compiler_params=_params(("parallel", "parallel", "arbitrary"), blocks, temp_bytes=16 * _nbytes((tq, tq), F32)),
        name="sb_prompt",
    )(q, k, v, u)


def _sb_paged_kernel(pt_ref, q_ref, kn_ref, vn_ref, u_ref, *refs, n_new, n_heads, pages_per_step):
    del pt_ref
    k_refs = refs[:pages_per_step]
    v_refs = refs[pages_per_step:2 * pages_per_step]
    o_ref = refs[2 * pages_per_step]
    qexp_ref, acc_ref, carry_ref = refs[2 * pages_per_step + 1:]
    j = pl.program_id(1)
    d = q_ref.shape[2]
    n_rows = n_new * n_heads
    own_lanes = (lax.broadcasted_iota(jnp.int32, (n_heads, d), 0)
                 == lax.broadcasted_iota(jnp.int32, (n_heads, d), 1) // (d // n_heads))

    @pl.when(j == 0)
    def _():
        q = q_ref[0]
        qexp = jnp.concatenate(
            [jnp.where(own_lanes, jnp.broadcast_to(q[t:t + 1, :], (n_heads, d)), 0.0) for t in range(n_new)], axis=0)
        qexp_ref[...] = qexp.astype(BF16)
        row_t = lax.broadcasted_iota(jnp.int32, (n_rows, 1), 0) // n_heads
        kn = kn_ref[0]
        vn = vn_ref[0]
        carry = jnp.zeros((n_rows, 1), F32)
        acc = jnp.zeros((n_rows, d), F32)
        for t_key in reversed(range(n_new)):
            valid = t_key < row_t
            s = jnp.sum(qexp * kn[t_key:t_key + 1, :], -1, keepdims=True)
            sp = _softplus(s)
            w = jnp.where(valid, jnp.exp(s - sp + carry), 0.0)
            acc = acc + w * vn[t_key:t_key + 1, :]
            carry = carry + jnp.where(valid, -sp, 0.0)
        acc_ref[...] = acc
        carry_ref[...] = jnp.broadcast_to(carry, carry_ref.shape)

    qexp_b = qexp_ref[...]
    s = jnp.concatenate([_dot_nt(qexp_b, k_refs[p][0].astype(BF16)) for p in range(pages_per_step)], axis=1)
    sp = _softplus(s)
    l = -sp
    page = k_refs[0].shape[1]
    carry = carry_ref[:, 0:1]
    w = jnp.exp(s - sp + _suffix_sum(l, u_ref[...]) + carry).astype(BF16)
    acc = acc_ref[...]
    for p in range(pages_per_step):
        acc = acc + _dot(w[:, p * page:(p + 1) * page], v_refs[p][0].astype(BF16))
    acc_ref[...] = acc
    carry_ref[...] = jnp.broadcast_to(carry + jnp.sum(l, -1, keepdims=True), carry_ref.shape)

    @pl.when(j == pl.num_programs(1) - 1)
    def _():
        o_ref[...] = jnp.zeros_like(o_ref)
        res = acc_ref[...]
        for t in range(n_new):
            blk = jnp.where(own_lanes, res[t * n_heads:(t + 1) * n_heads, :], 0.0)
            o_ref[0, t:t + 1, :] = jnp.sum(blk, 0, keepdims=True).astype(o_ref.dtype)


def sb_paged(q, k_new, v_new, cache_k, cache_v, page_table, n_new, pages_per_step=4):
    n_seq, tp, d = q.shape
    n_pages = page_table.shape[1]
    page = cache_k.shape[1]
    n_steps = n_pages // pages_per_step
    n_rows = n_new * C_HEADS
    blk_keys = pages_per_step * page
    u = jnp.tril(jnp.ones((blk_keys, blk_keys), BF16), k=-1)
    pt_flat = page_table.reshape(-1)

    def page_map(p):
        return lambda s, j, pt: (pt[s * n_pages + (n_steps - 1 - j) * pages_per_step + p], 0, 0)

    seq3 = lambda s, j, pt: (s, 0, 0)
    blocks = 4 * _nbytes((tp, d), F32) + _nbytes(u.shape, BF16) + 2 * pages_per_step * _nbytes((page, d), F32)
    scratch = _nbytes((n_rows, d), BF16) + _nbytes((n_rows, d), F32) + _nbytes((n_rows, V7X_LANES), F32)
    grid_spec = pltpu.PrefetchScalarGridSpec(
        num_scalar_prefetch=1,
        grid=(n_seq, n_steps),
        in_specs=[pl.BlockSpec((1, tp, d), seq3), pl.BlockSpec((1, tp, d), seq3), pl.BlockSpec((1, tp, d), seq3),
                  pl.BlockSpec(u.shape, lambda s, j, pt: (0, 0))]
                 + [pl.BlockSpec((1, page, d), page_map(p)) for p in range(pages_per_step)]
                 + [pl.BlockSpec((1, page, d), page_map(p)) for p in range(pages_per_step)],
        out_specs=pl.BlockSpec((1, tp, d), seq3),
        scratch_shapes=[pltpu.VMEM((n_rows, d), BF16), pltpu.VMEM((n_rows, d), F32),
                        pltpu.VMEM((n_rows, V7X_LANES), F32)],
    )
    return pl.pallas_call(
        functools.partial(_sb_paged_kernel, n_new=n_new, n_heads=C_HEADS, pages_per_step=pages_per_step),
        grid_spec=grid_spec,
        out_shape=jax.ShapeDtypeStruct((n_seq, tp, d), F32),
        compiler_params=_params(("parallel", "arbitrary"), blocks, scratch, temp_bytes=8 << 20),
        name="sb_paged",
    )(pt_flat, q, k_new, v_new, u, *([cache_k] * pages_per_step), *([cache_v] * pages_per_step))


def _trunk(xf, n_seq, t_pad, t_valid, a_conv, a_ssm, b_conv, mem, past, weights, tm_tok, tm_seq, sb_tq):
    (a_wqkvz, a_wab, a_conv_w, a_alog, a_dtb, a_norm_w, a_w_out, b_w_in, b_conv_w, b_w_out,
     c_w_qkv, c_b_qkv, c_w_out, m_w_q, m_w_o, f_w1, f_w2, ln_g, ln_b, alpha) = weights
    m, d = xf.shape
    depth = ln_g.shape[0]
    xb = xf.astype(BF16)
    zero_bias = jnp.zeros((1, d), F32)
    new_a_conv, new_a_ssm, new_b_conv, new_c_k, new_c_v = [], [], [], [], []
    t_valid_last = t_valid - (t_pad - tm_seq)
    for i in range(depth):
        kind, j = i % 3, i // 3
        g = lambda n: ln_g[i, n][None]
        b = lambda n: ln_b[i, n][None]
        if kind == 0:
            q, k, v, z, gb, cso = a_prep(xf, a_wqkvz[j], a_wab[j], a_conv_w[j], a_alog[j], a_dtb[j], a_conv[j],
                                        n_seq, tm_seq, t_valid_last)
            new_a_conv.append(cso)
            if t_pad % A_CHUNK:
                pad = lambda a: jnp.pad(a.reshape(n_seq, t_pad, -1), ((0, 0), (0, A_CHUNK - t_pad), (0, 0))
                                        ).reshape(n_seq * A_CHUNK, -1)
                o, s_new = gdn(pad(q), pad(k), pad(v), pad(z), pad(gb), a_ssm[j], a_norm_w[j], n_seq, t_valid)
                o = o.reshape(n_seq, A_CHUNK, d)[:, :t_pad].reshape(m, d)
            else:
                o, s_new = gdn(q, k, v, z, gb, a_ssm[j], a_norm_w[j], n_seq, A_CHUNK)
            new_a_ssm.append(s_new)
            xf, xb = mm_ln(o, a_w_out[j], xf, g(0), b(0), alpha, tm_tok)
        elif kind == 1:
            xf, xb, cso = b_mixer(xf, b_w_in[j], b_conv_w[j], b_w_out[j], b_conv[j], g(0), b(0), alpha,
                                  n_seq, tm_seq, t_valid_last)
            new_b_conv.append(cso)
        else:
            dh = d // C_HEADS
            if past is None:
                qs, kf, kb16, vf, vb16 = mm_split(xb, c_w_qkv[j], c_b_qkv[j], [[BF16], [F32, BF16], [F32, BF16]],
                                                  [dh ** -0.5, 1.0, 1.0], tm_tok)
                o = sb_prompt(qs, kb16, vb16, n_seq, sb_tq)
            else:
                qs, kf, vf = mm_split(xb, c_w_qkv[j], c_b_qkv[j], [[F32], [F32], [F32]], [dh ** -0.5, 1.0, 1.0],
                                      tm_tok)
                r3 = lambda a: a.reshape(n_seq, t_pad, d)
                o = sb_paged(r3(qs), r3(kf), r3(vf), past[0][j], past[1][j], past[2], t_valid)
                o = o.reshape(m, d).astype(BF16)
            new_c_k.append(kf)
            new_c_v.append(vf)
            xf, xb = mm_ln(o, c_w_out[j], xf, g(0), b(0), alpha, tm_tok)
        mk, mv = mem[i]
        if past is None:
            xf, xb = memattn_prompt(xb, xf, mk, mv, m_w_q[i], m_w_o[i], g(1), b(1), alpha, n_seq, tm_seq)
        else:
            (qm,) = mm_split(xb, m_w_q[i], zero_bias, [[F32]], [(d // M_HEADS) ** -0.5], tm_tok)
            o = memattn_sample_core(qm.reshape(n_seq, t_pad, d), mk, mv).reshape(m, d).astype(BF16)
            xf, xb = mm_ln(o, m_w_o[i], xf, g(1), b(1), alpha, tm_tok)
        xf, xb = mlp_ln(xb, f_w1[i], f_w2[i], xf, g(2), b(2), alpha, tm_tok)
    return xf, new_a_conv, new_a_ssm, new_b_conv, new_c_k, new_c_v


def kernel(x_prompt, x_sample, state_a_conv, state_a_ssm, state_b_conv, cache_c_k, cache_c_v, cache_mem_k, cache_mem_v, page_table, mem_prompt, a_w_in, a_conv_w, a_a_log, a_dt_bias, a_norm_w, a_w_out, b_w_in, b_conv_w, b_w_out, c_w_qkv, c_b_qkv, c_w_out, m_w_q, m_w_kv, m_w_o, f_w1, f_w2, ln_g, ln_b):
    bp, t_p, d = x_prompt.shape
    bs, t_s, _ = x_sample.shape
    depth = ln_g.shape[0]
    alpha = (2.0 * depth) ** 0.25
    n_a, n_b, n_c = a_w_in.shape[0], b_w_in.shape[0], c_w_qkv.shape[0]
    a_qkv = a_conv_w.shape[2]
    n_mem = mem_prompt.shape[1]
    bf = lambda a: a.astype(BF16)

    a_wqkvz = bf(a_w_in[:, :, :a_qkv + d])
    ab_cols = a_w_in.shape[2] - (a_qkv + d)
    a_wab = bf(jnp.pad(a_w_in[:, :, a_qkv + d:], ((0, 0), (0, 0), (0, V7X_LANES - ab_cols))))
    lane_pad = lambda a: jnp.pad(a, ((0, 0), (0, V7X_LANES - a.shape[1])))[:, None, :]
    weights = (a_wqkvz, a_wab, a_conv_w, lane_pad(a_a_log), lane_pad(a_dt_bias), a_norm_w[:, None, :], bf(a_w_out),
               bf(b_w_in), b_conv_w, bf(b_w_out), bf(c_w_qkv), c_b_qkv.reshape(n_c, 1, 3 * d), bf(c_w_out),
               bf(m_w_q), bf(m_w_o), bf(f_w1), bf(f_w2), ln_g, ln_b, alpha)

    mp = bf(mem_prompt.reshape(bp * n_mem, d))
    zero_kv_bias = jnp.zeros((1, 2 * d), F32)
    w_kv = bf(m_w_kv)
    p_mem, p_mem_k, p_mem_v = [], [], []
    for i in range(depth):
        kf, kb16, vf, vb16 = mm_split(mp, w_kv[i], zero_kv_bias, [[F32, BF16], [F32, BF16]], [1.0, 1.0],
                                      min(PROMPT_TILE, bp * n_mem))
        p_mem.append((kb16, vb16))
        p_mem_k.append(kf)
        p_mem_v.append(vf)
    dm = d // M_HEADS
    p_mem_k = jnp.stack(p_mem_k).reshape(depth, bp, n_mem, M_HEADS, dm)
    p_mem_v = jnp.stack(p_mem_v).reshape(depth, bp, n_mem, M_HEADS, dm)
    yp, pa_conv, pa_ssm, pb_conv, pc_k, pc_v = _trunk(
        x_prompt.reshape(bp * t_p, d), bp, t_p, t_p,
        jnp.zeros((n_a, bp) + state_a_conv.shape[2:], F32), jnp.zeros((n_a, bp) + state_a_ssm.shape[2:], F32),
        jnp.zeros((n_b, bp) + state_b_conv.shape[2:], F32), p_mem, None, weights,
        tm_tok=PROMPT_TILE, tm_seq=min(PROMPT_TILE, t_p), sb_tq=min(PROMPT_TILE, t_p))

    tp = SAMPLE_T_PAD
    xs = jnp.pad(x_sample, ((0, 0), (0, tp - t_s), (0, 0))).reshape(bs * tp, d)
    page = cache_c_k.shape[2]
    s_mem = [(cache_mem_k[i].reshape(bs, n_mem, d), cache_mem_v[i].reshape(bs, n_mem, d)) for i in range(depth)]
    past = (cache_c_k.reshape(n_c, -1, page, d), cache_c_v.reshape(n_c, -1, page, d), page_table)
    ys, sa_conv, sa_ssm, sb_conv, sc_k, sc_v = _trunk(
        xs, bs, tp, t_s, state_a_conv, state_a_ssm, state_b_conv, s_mem, past, weights,
        tm_tok=bs * tp, tm_seq=tp, sb_tq=None)

    dh = d // C_HEADS
    unpad = lambda a: a.reshape(bs, tp, -1)[:, :t_s]
    return (yp.reshape(bp, t_p, d), unpad(ys),
            jnp.stack(pa_conv), jnp.stack(pa_ssm), jnp.stack(pb_conv),
            jnp.stack(pc_k).reshape(n_c, bp, t_p, C_HEADS, dh), jnp.stack(pc_v).reshape(n_c, bp, t_p, C_HEADS, dh),
            p_mem_k, p_mem_v,
            jnp.stack(sa_conv), jnp.stack(sa_ssm), jnp.stack(sb_conv),
            jnp.stack([unpad(a) for a in sc_k]).reshape(n_c, bs, t_s, C_HEADS, dh),
            jnp.stack([unpad(a) for a in sc_v]).reshape(n_c, bs, t_s, C_HEADS, dh))
```

```python
import functools
import math

import jax
import jax.numpy as jnp
from jax import lax
from jax.experimental import pallas as pl
from jax.experimental.pallas import tpu as pltpu

F32 = jnp.float32
BF16 = jnp.bfloat16
HIGHEST = lax.Precision.HIGHEST

V7X_LANES = 128
V7X_SUBLANES = 8
V7X_VMEM_BYTES = 64 * 1024 * 1024
VMEM_CAP_BYTES = V7X_VMEM_BYTES - 8 * 1024 * 1024

LN_EPS = 1e-5
NORM_EPS = 1e-6
A_CHUNK = 64
A_HEADS = 8
A_DK = 128
C_HEADS = 16
M_HEADS = 4
SAMPLE_T_PAD = V7X_SUBLANES
PROMPT_TILE = 256


def _vmem_limit(block_bytes, scratch_bytes=0, temp_bytes=0):
    est = 2 * block_bytes + scratch_bytes + temp_bytes + (4 << 20)
    return int(min(max(est, 16 << 20), VMEM_CAP_BYTES))


def _nbytes(shape, dtype):
    return math.prod(shape) * jnp.dtype(dtype).itemsize


def _params(sem, block_bytes, scratch_bytes=0, temp_bytes=0):
    return pltpu.CompilerParams(dimension_semantics=sem,
                                vmem_limit_bytes=_vmem_limit(block_bytes, scratch_bytes, temp_bytes))


def _layer_norm(v, g, b):
    mu = jnp.mean(v, -1, keepdims=True)
    c = v - mu
    var = jnp.mean(c * c, -1, keepdims=True)
    return c * lax.rsqrt(var + LN_EPS) * g + b


def _softplus(x):
    return jnp.maximum(x, 0.0) + jnp.log1p(jnp.exp(-jnp.abs(x)))


def _log_sigmoid_neg(x):
    return jnp.minimum(-x, 0.0) - jnp.log(1.0 + jnp.exp(-jnp.abs(x)))


def _silu(x):
    return x * jax.nn.sigmoid(x)


def _dot(a, b, **kw):
    return jnp.dot(a, b, preferred_element_type=F32, **kw)


def _dot_nt(a, b):
    return lax.dot_general(a, b, (((1,), (1,)), ((), ())), preferred_element_type=F32)


def _bmm(a, b):
    return lax.dot_general(a, b, (((2,), (1,)), ((0,), (0,))), preferred_element_type=F32)


def _bmm_nt(a, b):
    return lax.dot_general(a, b, (((2,), (2,)), ((0,), (0,))), preferred_element_type=F32)


def _mm_split_kernel(x_ref, w_ref, b_ref, *o_refs, d, scales, n_copies):
    x = x_ref[...]
    k = 0
    for j, n_out in enumerate(n_copies):
        cols = slice(j * d, (j + 1) * d)
        y = _dot(x, w_ref[:, cols]) + b_ref[:, cols]
        if scales[j] != 1.0:
            y = y * scales[j]
        for _ in range(n_out):
            o_refs[k][...] = y.astype(o_refs[k].dtype)
            k += 1


def mm_split(x, w, bias, out_dtypes, scales, tm):
    m, kdim = x.shape
    ns = len(out_dtypes)
    d = w.shape[1] // ns
    flat = [dt for group in out_dtypes for dt in group]
    blocks = (_nbytes((tm, kdim), x.dtype) + _nbytes(w.shape, w.dtype) + _nbytes(bias.shape, F32)
              + sum(_nbytes((tm, d), dt) for dt in flat))
    return pl.pallas_call(
        functools.partial(_mm_split_kernel, d=d, scales=tuple(scales), n_copies=tuple(len(g) for g in out_dtypes)),
        grid=(m // tm,),
        in_specs=[pl.BlockSpec((tm, kdim), lambda i: (i, 0)),
                  pl.BlockSpec(w.shape, lambda i: (0, 0)),
                  pl.BlockSpec(bias.shape, lambda i: (0, 0))],
        out_specs=[pl.BlockSpec((tm, d), lambda i: (i, 0)) for _ in flat],
        out_shape=[jax.ShapeDtypeStruct((m, d), dt) for dt in flat],
        compiler_params=_params(("parallel",), blocks, temp_bytes=2 * _nbytes((tm, d), F32)),
        name="mm_split",
    )(x, w, bias)


def _mm_ln_kernel(x_ref, w_ref, res_ref, g_ref, b_ref, of_ref, ob_ref, *, alpha):
    y = _dot(x_ref[...], w_ref[...])
    v = _layer_norm(alpha * res_ref[...] + y, g_ref[...], b_ref[...])
    of_ref[...] = v
    ob_ref[...] = v.astype(BF16)


def mm_ln(x, w, res, g, b, alpha, tm):
    m, kdim = x.shape
    d = w.shape[1]
    blocks = (_nbytes((tm, kdim), BF16) + _nbytes(w.shape, BF16) + 2 * _nbytes((tm, d), F32)
              + _nbytes((tm, d), BF16))
    row = lambda i: (i, 0)
    const = lambda i: (0, 0)
    return pl.pallas_call(
        functools.partial(_mm_ln_kernel, alpha=alpha),
        grid=(m // tm,),
        in_specs=[pl.BlockSpec((tm, kdim), row), pl.BlockSpec(w.shape, const), pl.BlockSpec((tm, d), row),
                  pl.BlockSpec((1, d), const), pl.BlockSpec((1, d), const)],
        out_specs=[pl.BlockSpec((tm, d), row), pl.BlockSpec((tm, d), row)],
        out_shape=[jax.ShapeDtypeStruct((m, d), F32), jax.ShapeDtypeStruct((m, d), BF16)],
        compiler_params=_params(("parallel",), blocks, temp_bytes=3 * _nbytes((tm, d), F32)),
        name="mm_ln",
    )(x, w, res, g, b)


def _mlp_kernel(x_ref, w1_ref, w2_ref, res_ref, g_ref, b_ref, of_ref, ob_ref, *, alpha, tf):
    x = x_ref[...]
    acc = None
    for c in range(w1_ref.shape[1] // tf):
        cols = slice(c * tf, (c + 1) * tf)
        h = jnp.square(jnp.maximum(_dot(x, w1_ref[:, cols]), 0.0)).astype(BF16)
        part = _dot(h, w2_ref[cols, :])
        acc = part if acc is None else acc + part
    v = _layer_norm(alpha * res_ref[...] + acc, g_ref[...], b_ref[...])
    of_ref[...] = v
    ob_ref[...] = v.astype(BF16)


def mlp_ln(x, w1, w2, res, g, b, alpha, tm, tf=1024):
    m, d = x.shape
    blocks = (_nbytes((tm, d), BF16) + _nbytes(w1.shape, BF16) + _nbytes(w2.shape, BF16)
              + 2 * _nbytes((tm, d), F32) + _nbytes((tm, d), BF16))
    row = lambda i: (i, 0)
    const = lambda i: (0, 0)
    return pl.pallas_call(
        functools.partial(_mlp_kernel, alpha=alpha, tf=tf),
        grid=(m // tm,),
        in_specs=[pl.BlockSpec((tm, d), row), pl.BlockSpec(w1.shape, const), pl.BlockSpec(w2.shape, const),
                  pl.BlockSpec((tm, d), row), pl.BlockSpec((1, d), const), pl.BlockSpec((1, d), const)],
        out_specs=[pl.BlockSpec((tm, d), row), pl.BlockSpec((tm, d), row)],
        out_shape=[jax.ShapeDtypeStruct((m, d), F32), jax.ShapeDtypeStruct((m, d), BF16)],
        compiler_params=_params(("parallel",), blocks, temp_bytes=3 * _nbytes((tm, tf), F32) + 2 * _nbytes((tm, d), F32)),
        name="mlp_ln",
    )(x, w1, w2, res, g, b)


def _memattn_kernel(x_ref, res_ref, mk_ref, mv_ref, wq_ref, wo_ref, g_ref, b_ref, of_ref, ob_ref, *, alpha, n_heads):
    d = x_ref.shape[1]
    dh = d // n_heads
    q = (_dot(x_ref[...], wq_ref[...]) * (dh ** -0.5)).astype(BF16)
    heads = []
    for h in range(n_heads):
        cols = slice(h * dh, (h + 1) * dh)
        s = _dot_nt(q[:, cols], mk_ref[:, cols])
        e = jnp.exp(s - jnp.max(s, -1, keepdims=True))
        p = e / jnp.sum(e, -1, keepdims=True)
        heads.append(_dot(p.astype(BF16), mv_ref[:, cols]).astype(BF16))
    o = jnp.concatenate(heads, axis=1)
    v = _layer_norm(alpha * res_ref[...] + _dot(o, wo_ref[...]), g_ref[...], b_ref[...])
    of_ref[...] = v
    ob_ref[...] = v.astype(BF16)


def memattn_prompt(xb, xf, mk, mv, wq, wo, g, b, alpha, n_seq, tm):
    m, d = xb.shape
    n_t = m // n_seq // tm
    n_mem = mk.shape[0] // n_seq
    blocks = (_nbytes((tm, d), BF16) * 2 + _nbytes((tm, d), F32) * 2 + 2 * _nbytes((n_mem, d), BF16)
              + 2 * _nbytes(wq.shape, BF16))
    row = lambda s, t: (s * n_t + t, 0)
    seq = lambda s, t: (s, 0)
    const = lambda s, t: (0, 0)
    return pl.pallas_call(
        functools.partial(_memattn_kernel, alpha=alpha, n_heads=M_HEADS),
        grid=(n_seq, n_t),
        in_specs=[pl.BlockSpec((tm, d), row), pl.BlockSpec((tm, d), row),
                  pl.BlockSpec((n_mem, d), seq), pl.BlockSpec((n_mem, d), seq),
                  pl.BlockSpec(wq.shape, const), pl.BlockSpec(wo.shape, const),
                  pl.BlockSpec((1, d), const), pl.BlockSpec((1, d), const)],
        out_specs=[pl.BlockSpec((tm, d), row), pl.BlockSpec((tm, d), row)],
        out_shape=[jax.ShapeDtypeStruct((m, d), F32), jax.ShapeDtypeStruct((m, d), BF16)],
        compiler_params=_params(("parallel", "parallel"), blocks, temp_bytes=6 * _nbytes((tm, d), F32)),
        name="memattn_prompt",
    )(xb, xf, mk, mv, wq, wo, g, b)


def _head_lane_mask(n_heads, d):
    lane_head = lax.broadcasted_iota(jnp.int32, (1, d), 1) // (d // n_heads)
    return [lane_head == h for h in range(n_heads)]


def _memattn_sample_kernel(q_ref, mk_ref, mv_ref, o_ref, *, n_heads):
    q = q_ref[0]
    d = q.shape[1]
    masks = _head_lane_mask(n_heads, d)
    qexp = jnp.concatenate([jnp.where(mk_h, q, 0.0) for mk_h in masks], axis=0).astype(BF16)
    s = _dot_nt(qexp, mk_ref[0].astype(BF16))
    e = jnp.exp(s - jnp.max(s, -1, keepdims=True))
    p = e / jnp.sum(e, -1, keepdims=True)
    o_all = _dot(p.astype(BF16), mv_ref[0].astype(BF16))
    tp = q.shape[0]
    out = jnp.zeros_like(q)
    for h in range(n_heads):
        out = out + jnp.where(masks[h], o_all[h * tp:(h + 1) * tp, :], 0.0)
    o_ref[0] = out.astype(o_ref.dtype)


def memattn_sample_core(q, mk, mv):
    n_seq, tp, d = q.shape
    n_mem = mk.shape[1]
    blocks = 2 * _nbytes((tp, d), F32) + 2 * _nbytes((n_mem, d), F32)
    return pl.pallas_call(
        functools.partial(_memattn_sample_kernel, n_heads=M_HEADS),
        grid=(n_seq,),
        in_specs=[pl.BlockSpec((1, tp, d), lambda s: (s, 0, 0)),
                  pl.BlockSpec((1, n_mem, d), lambda s: (s, 0, 0)),
                  pl.BlockSpec((1, n_mem, d), lambda s: (s, 0, 0))],
        out_specs=pl.BlockSpec((1, tp, d), lambda s: (s, 0, 0)),
        out_shape=jax.ShapeDtypeStruct((n_seq, tp, d), F32),
        compiler_params=_params(("parallel",), blocks, temp_bytes=2 * _nbytes((n_mem, d), F32)),
        name="memattn_sample",
    )(q, mk, mv)


A_HALO_ROW = V7X_SUBLANES


def _a_prep_kernel(x_ref, wqkvz_ref, wab_ref, cw_ref, alog_ref, dtb_ref, cs_ref,
                   q_ref, k_ref, v_ref, z_ref, gb_ref, cso_ref, xp_ref, *, tm, n_t, t_valid_last, n_heads):
    t = pl.program_id(1)
    width = cw_ref.shape[0]
    halo = width - 1
    r0 = A_HALO_ROW
    d = q_ref.shape[1]
    dk = d // n_heads

    @pl.when(t == 0)
    def _():
        xp_ref[r0 - halo:r0, :] = cs_ref[0]

    @pl.when(t > 0)
    def _():
        xp_ref[r0 - halo:r0, :] = xp_ref[r0 + tm - halo:r0 + tm, :]

    x = x_ref[...].astype(BF16)
    for seg, o_ref in enumerate((q_ref, k_ref, v_ref)):
        cols = slice(seg * d, (seg + 1) * d)
        xp_ref[r0:r0 + tm, cols] = _dot(x, wqkvz_ref[:, cols])
        y = xp_ref[r0 - halo:r0 - halo + tm, cols] * cw_ref[0:1, cols]
        for i in range(1, width):
            y = y + xp_ref[r0 - halo + i:r0 - halo + i + tm, cols] * cw_ref[i:i + 1, cols]
        y = _silu(y)
        if seg < 2:
            parts = []
            for h in range(n_heads):
                yh = y[:, h * dk:(h + 1) * dk]
                yh = yh * lax.rsqrt(jnp.sum(yh * yh, -1, keepdims=True) + NORM_EPS)
                parts.append(yh * (dk ** -0.5) if seg == 0 else yh)
            y = jnp.concatenate(parts, axis=1)
        o_ref[...] = y
    z_ref[...] = _dot(x, wqkvz_ref[:, 3 * d:4 * d])
    ab = _dot(x, wab_ref[...])
    g = -jnp.exp(alog_ref[...]) * _softplus(ab + dtb_ref[...])
    lane = lax.broadcasted_iota(jnp.int32, ab.shape, 1)
    gb_ref[...] = jnp.where(lane < n_heads, g, jax.nn.sigmoid(ab))

    @pl.when(t == n_t - 1)
    def _():
        cso_ref[0] = xp_ref[r0 + t_valid_last - halo:r0 + t_valid_last, :]


def a_prep(xf, wqkvz, wab, conv_w, alog, dtb, conv_state, n_seq, tm, t_valid_last):
    m, d = xf.shape
    n_t = m // n_seq // tm
    halo = conv_w.shape[0] - 1
    cq = conv_w.shape[1]
    row = lambda s, t: (s * n_t + t, 0)
    const = lambda s, t: (0, 0)
    seq3 = lambda s, t: (s, 0, 0)
    blocks = (_nbytes((tm, d), F32) + _nbytes(wqkvz.shape, BF16) + _nbytes(wab.shape, BF16)
              + 4 * _nbytes((tm, d), F32) + _nbytes((tm, V7X_LANES), F32) + 2 * _nbytes((V7X_SUBLANES, cq), F32))
    scratch = _nbytes((A_HALO_ROW + tm, cq), F32)
    outs = pl.pallas_call(
        functools.partial(_a_prep_kernel, tm=tm, n_t=n_t, t_valid_last=t_valid_last, n_heads=A_HEADS),
        grid=(n_seq, n_t),
        in_specs=[pl.BlockSpec((tm, d), row), pl.BlockSpec(wqkvz.shape, const), pl.BlockSpec(wab.shape, const),
                  pl.BlockSpec(conv_w.shape, const), pl.BlockSpec(alog.shape, const), pl.BlockSpec(dtb.shape, const),
                  pl.BlockSpec((1, halo, cq), seq3)],
        out_specs=[pl.BlockSpec((tm, d), row)] * 4 + [pl.BlockSpec((tm, V7X_LANES), row),
                                                      pl.BlockSpec((1, halo, cq), seq3)],
        out_shape=[jax.ShapeDtypeStruct((m, d), F32)] * 4 + [jax.ShapeDtypeStruct((m, V7X_LANES), F32),
                                                             jax.ShapeDtypeStruct((n_seq, halo, cq), F32)],
        scratch_shapes=[pltpu.VMEM((A_HALO_ROW + tm, cq), F32)],
        compiler_params=_params(("parallel", "arbitrary"), blocks, scratch, temp_bytes=6 * _nbytes((tm, d), F32)),
        name="a_prep",
    )(xf, wqkvz, wab, conv_w, alog, dtb, conv_state)
    return outs


def _gdn_kernel(q_ref, k_ref, v_ref, z_ref, gb_ref, s0_ref, nw_ref, o_ref, s_ref, *, t_valid, n_heads):
    c = pl.program_id(1)
    ln = q_ref.shape[0]
    dk = q_ref.shape[1] // n_heads

    @pl.when(c == 0)
    def _():
        s_ref[...] = s0_ref[...]

    row = lax.broadcasted_iota(jnp.int32, (ln, ln), 0)
    col = lax.broadcasted_iota(jnp.int32, (ln, ln), 1)
    tri_incl = row >= col
    strict = row > col
    eye = (row == col).astype(F32)
    gb = gb_ref[...]
    row_valid = None
    if t_valid < ln:
        row_valid = lax.broadcasted_iota(jnp.int32, (ln, 1), 0) < t_valid
        gb = jnp.where(row_valid, gb, 0.0)
    g_cum = _dot(tri_incl.astype(F32), gb, precision=HIGHEST)
    g_cum_t = g_cum.T
    g_last = g_cum[ln - 1:ln, :]

    def heads(x):
        return jnp.stack([x[:, h * dk:(h + 1) * dk] for h in range(n_heads)])

    def head_cols(x, first):
        return jnp.stack([x[:, first + h:first + h + 1] for h in range(n_heads)])

    beta = head_cols(gb, n_heads)
    g_col = head_cols(g_cum, 0)
    e_cum = jnp.exp(g_col)
    e_rem = jnp.exp(head_cols(g_last, 0) - g_col)
    e_last = jnp.exp(head_cols(g_last, 0))
    g_row = jnp.stack([g_cum_t[h:h + 1, :] for h in range(n_heads)])
    decay = jnp.where(tri_incl, jnp.exp(g_col - g_row), 0.0)
    q3 = heads(q_ref[...])
    k3 = heads(k_ref[...])
    v3 = heads(v_ref[...])
    if row_valid is not None:
        k3 = jnp.where(row_valid, k3, 0.0)
    k3_b = k3.astype(BF16)
    kb = k3 * beta
    m_mat = jnp.where(strict, _bmm_nt(kb.astype(BF16), k3_b) * decay, 0.0)
    neg = -m_mat
    inv = eye + neg
    pw = neg
    for _ in range(int(math.log2(ln)) - 1):
        pw_b = pw.astype(BF16)
        pw = _bmm(pw_b, pw_b)
        inv = inv + _bmm(inv.astype(BF16), pw.astype(BF16))
    rhs = jnp.concatenate([v3 * beta, kb * e_cum], axis=2)
    sol = _bmm(inv.astype(BF16), rhs.astype(BF16))
    u = sol[:, :, :dk]
    w = sol[:, :, dk:]
    s_all = s_ref[0]
    s_b = s_all.astype(BF16)
    v_new = u - _bmm(w.astype(BF16), s_b)
    v_new_b = v_new.astype(BF16)
    intra = _bmm_nt(q3.astype(BF16), k3_b) * decay
    o = _bmm((q3 * e_cum).astype(BF16), s_b) + _bmm(intra.astype(BF16), v_new_b)
    kd_t = jnp.swapaxes(k3 * e_rem, 1, 2).astype(BF16)
    s_ref[0] = s_all * e_last + _bmm(kd_t, v_new_b)
    o = o * lax.rsqrt(jnp.mean(o * o, -1, keepdims=True) + NORM_EPS) * nw_ref[...] * _silu(heads(z_ref[...]))
    for h in range(n_heads):
        o_ref[:, h * dk:(h + 1) * dk] = o[h].astype(o_ref.dtype)


def gdn(q, k, v, z, gb, s0, norm_w, n_seq, t_valid):
    m, d = q.shape
    ln = A_CHUNK
    n_c = m // n_seq // ln
    row = lambda s, c: (s * n_c + c, 0)
    const = lambda s, c: (0, 0)
    seq4 = lambda s, c: (s, 0, 0, 0)
    sblock = (1,) + s0.shape[1:]
    blocks = (4 * _nbytes((ln, d), F32) + _nbytes((ln, V7X_LANES), F32) + 2 * _nbytes(sblock, F32)
              + _nbytes((ln, d), BF16))
    return pl.pallas_call(
        functools.partial(_gdn_kernel, t_valid=t_valid, n_heads=A_HEADS),
        grid=(n_seq, n_c),
        in_specs=[pl.BlockSpec((ln, d), row)] * 4 + [pl.BlockSpec((ln, V7X_LANES), row),
                                                     pl.BlockSpec(sblock, seq4), pl.BlockSpec(norm_w.shape, const)],
        out_specs=[pl.BlockSpec((ln, d), row), pl.BlockSpec(sblock, seq4)],
        out_shape=[jax.ShapeDtypeStruct((m, d), BF16), jax.ShapeDtypeStruct(s0.shape, F32)],
        compiler_params=_params(("parallel", "arbitrary"), blocks, temp_bytes=8 << 20),
        name="gdn",
    )(q, k, v, z, gb, s0, norm_w)


def _b_mixer_kernel(res_ref, win_ref, cw_ref, wout_ref, cs_ref, g_ref, b_ref,
                    of_ref, ob_ref, cso_ref, xp_ref, *, alpha, tm, n_t, t_valid_last):
    t = pl.program_id(1)
    width = cw_ref.shape[0]
    halo = width - 1
    r0 = A_HALO_ROW
    d = res_ref.shape[1]

    @pl.when(t == 0)
    def _():
        xp_ref[r0 - halo:r0, :] = cs_ref[0]

    @pl.when(t > 0)
    def _():
        xp_ref[r0 - halo:r0, :] = xp_ref[r0 + tm - halo:r0 + tm, :]

    x = res_ref[...].astype(BF16)
    gate_c = _dot(x, win_ref[:, d:2 * d])
    hid = _dot(x, win_ref[:, 2 * d:3 * d])
    xp_ref[r0:r0 + tm, :] = gate_c * hid
    y = xp_ref[r0 - halo:r0 - halo + tm, :] * cw_ref[0:1, :]
    for i in range(1, width):
        y = y + xp_ref[r0 - halo + i:r0 - halo + i + tm, :] * cw_ref[i:i + 1, :]
    gate_b = _dot(x, win_ref[:, 0:d])
    out = _dot((gate_b * y).astype(BF16), wout_ref[...])
    v = _layer_norm(alpha * res_ref[...] + out, g_ref[...], b_ref[...])
    of_ref[...] = v
    ob_ref[...] = v.astype(BF16)

    @pl.when(t == n_t - 1)
    def _():
        cso_ref[0] = xp_ref[r0 + t_valid_last - halo:r0 + t_valid_last, :]


def b_mixer(xf, w_in, conv_w, w_out, conv_state, g, b, alpha, n_seq, tm, t_valid_last):
    m, d = xf.shape
    n_t = m // n_seq // tm
    halo = conv_w.shape[0] - 1
    row = lambda s, t: (s * n_t + t, 0)
    const = lambda s, t: (0, 0)
    seq3 = lambda s, t: (s, 0, 0)
    blocks = (_nbytes((tm, d), BF16) + 2 * _nbytes((tm, d), F32) + _nbytes(w_in.shape, BF16)
              + _nbytes(w_out.shape, BF16) + 2 * _nbytes((V7X_SUBLANES, d), F32))
    scratch = _nbytes((A_HALO_ROW + tm, d), F32)
    return pl.pallas_call(
        functools.partial(_b_mixer_kernel, alpha=alpha, tm=tm, n_t=n_t, t_valid_last=t_valid_last),
        grid=(n_seq, n_t),
        in_specs=[pl.BlockSpec((tm, d), row), pl.BlockSpec(w_in.shape, const),
                  pl.BlockSpec(conv_w.shape, const), pl.BlockSpec(w_out.shape, const),
                  pl.BlockSpec((1, halo, d), seq3), pl.BlockSpec((1, d), const), pl.BlockSpec((1, d), const)],
        out_specs=[pl.BlockSpec((tm, d), row), pl.BlockSpec((tm, d), row), pl.BlockSpec((1, halo, d), seq3)],
        out_shape=[jax.ShapeDtypeStruct((m, d), F32), jax.ShapeDtypeStruct((m, d), BF16),
                   jax.ShapeDtypeStruct((n_seq, halo, d), F32)],
        scratch_shapes=[pltpu.VMEM((A_HALO_ROW + tm, d), F32)],
        compiler_params=_params(("parallel", "arbitrary"), blocks, scratch, temp_bytes=6 * _nbytes((tm, d), F32)),
        name="b_mixer",
    )(xf, w_in, conv_w, w_out, conv_state, g, b)


def _suffix_matrix(n):
    r = lax.broadcasted_iota(jnp.int32, (n, n), 0)
    c = lax.broadcasted_iota(jnp.int32, (n, n), 1)
    return (r > c).astype(BF16)


def _suffix_sum(l, u):
    l_hi = l.astype(BF16)
    l_lo = (l - l_hi.astype(F32)).astype(BF16)
    return _dot(l_hi, u) + _dot(l_lo, u)


def _sb_prompt_kernel(q_ref, k_ref, v_ref, u_ref, o_ref, *, tq, dh):
    i = pl.program_id(2)
    lanes = q_ref.shape[1]
    lane = lax.broadcasted_iota(jnp.int32, (1, lanes), 1)
    n_sub = lanes // dh
    q = q_ref[...]
    qs = [jnp.where(lane // dh == hh, q, jnp.zeros_like(q)) for hh in range(n_sub)]
    r = lax.broadcasted_iota(jnp.int32, (tq, tq), 0)
    c = lax.broadcasted_iota(jnp.int32, (tq, tq), 1)
    causal = c < r
    u = u_ref[...]

    def visit(j, state, diagonal):
        accs, carries = state
        start = pl.multiple_of(j * tq, tq)
        kj = k_ref[pl.ds(start, tq), :]
        vj = v_ref[pl.ds(start, tq), :]
        new_accs, new_carries = [], []
        for hh in range(n_sub):
            s = _dot_nt(qs[hh], kj)
            lsn = _log_sigmoid_neg(s)
            l = jnp.where(causal, lsn, 0.0) if diagonal else lsn
            w = jnp.exp(s + lsn + _suffix_sum(l, u) + carries[hh])
            if diagonal:
                w = jnp.where(causal, w, 0.0)
            new_accs.append(accs[hh] + _dot(w.astype(BF16), vj))
            new_carries.append(carries[hh] + jnp.sum(l, -1, keepdims=True))
        return tuple(new_accs), tuple(new_carries)

    init = (tuple(jnp.zeros((tq, lanes), F32) for _ in range(n_sub)),
            tuple(jnp.zeros((tq, 1), F32) for _ in range(n_sub)))
    state = visit(i, init, True)
    odd = lax.rem(i, 2)
    state = lax.cond(odd == 1, lambda st: visit(i - 1, st, False), lambda st: st, state)
    first = i - 1 - odd

    def pair(t, st):
        return visit(first - 2 * t - 1, visit(first - 2 * t, st, False), False)

    state = lax.fori_loop(0, lax.div(i, 2), pair, state)
    accs = state[0]
    out = accs[0]
    for hh in range(1, n_sub):
        out = jnp.where(lane // dh == hh, accs[hh], out)
    o_ref[...] = out.astype(o_ref.dtype)


def sb_prompt(q, k, v, n_seq, tq):
    m, d = q.shape
    t_len = m // n_seq
    n_q = t_len // tq
    dh = d // C_HEADS
    lanes = V7X_LANES
    u = jnp.tril(jnp.ones((tq, tq), BF16), k=-1)
    blocks = 2 * _nbytes((tq, lanes), BF16) + 2 * _nbytes((t_len, lanes), BF16) + _nbytes((tq, tq), BF16)
    return pl.pallas_call(
        functools.partial(_sb_prompt_kernel, tq=tq, dh=dh),
        grid=(n_seq, d // lanes, n_q),
        in_specs=[pl.BlockSpec((tq, lanes), lambda s, p, i: (s * n_q + i, p)),
                  pl.BlockSpec((t_len, lanes), lambda s, p, i: (s, p)),
                  pl.BlockSpec((t_len, lanes), lambda s, p, i: (s, p)),
                  pl.BlockSpec((tq, tq), lambda s, p, i: (0, 0))],
        out_specs=pl.BlockSpec((tq, lanes), lambda s, p, i: (s * n_q + i, p)),
        out_shape=jax.ShapeDtypeStruct((m, d), BF16),
        compiler_params=_params(("parallel", "parallel", "arbitrary"), blocks, temp_bytes=16 * _nbytes((tq, tq), F32)),
        name="sb_prompt",
    )(q, k, v, u)


def _sb_paged_kernel(pt_ref, q_ref, kn_ref, vn_ref, u_ref, *refs, n_new, n_heads, pages_per_step):
    del pt_ref
    k_refs = refs[:pages_per_step]
    v_refs = refs[pages_per_step:2 * pages_per_step]
    o_ref = refs[2 * pages_per_step]
    q3_ref, acc_ref, carry_ref = refs[2 * pages_per_step + 1:]
    j = pl.program_id(1)
    tp, d = q_ref.shape[1:]
    dh = d // n_heads
    page = k_refs[0].shape[3]
    n_rows = n_heads * tp

    def heads(x):
        return jnp.stack([x[:, h * dh:(h + 1) * dh] for h in range(n_heads)])

    @pl.when(j == 0)
    def _():
        q3 = heads(q_ref[0])
        q3_ref[...] = q3.astype(BF16)
        kn3 = heads(kn_ref[0])
        vn3 = heads(vn_ref[0])
        q_pos = lax.broadcasted_iota(jnp.int32, (1, tp, 1), 1)
        carry = jnp.zeros((n_heads, tp, 1), F32)
        acc = jnp.zeros((n_heads, tp, dh), F32)
        for t_key in reversed(range(n_new)):
            valid = t_key < q_pos
            s = jnp.sum(q3 * kn3[:, t_key:t_key + 1, :], -1, keepdims=True)
            lsn = _log_sigmoid_neg(s)
            w = jnp.where(valid, jnp.exp(s + lsn + carry), 0.0)
            acc = acc + w * vn3[:, t_key:t_key + 1, :]
            carry = carry + jnp.where(valid, lsn, 0.0)
        acc_ref[...] = acc
        carry_ref[...] = jnp.broadcast_to(carry.reshape(n_rows, 1), carry_ref.shape)

    q3_b = q3_ref[...]
    s = jnp.concatenate([_bmm(q3_b, k_refs[p][0].astype(BF16)) for p in range(pages_per_step)], axis=2)
    s = s.reshape(n_rows, pages_per_step * page)
    l = _log_sigmoid_neg(s)
    carry = carry_ref[:, 0:1]
    w = jnp.exp(s + l + _suffix_sum(l, u_ref[...]) + carry)
    w = w.reshape(n_heads, tp, pages_per_step * page).astype(BF16)
    acc = acc_ref[...]
    for p in range(pages_per_step):
        acc = acc + _bmm_nt(w[:, :, p * page:(p + 1) * page], v_refs[p][0].astype(BF16))
    acc_ref[...] = acc
    carry_ref[...] = jnp.broadcast_to(carry + jnp.sum(l, -1, keepdims=True), carry_ref.shape)

    @pl.when(j == pl.num_programs(1) - 1)
    def _():
        res = acc_ref[...]
        out = jnp.concatenate([res[h] for h in range(n_heads)], axis=1)
        is_new = lax.broadcasted_iota(jnp.int32, (tp, 1), 0) < n_new
        o_ref[0] = jnp.where(is_new, out, 0.0).astype(o_ref.dtype)


def sb_paged(q, k_new, v_new, cache_k, cache_v, page_table, n_new, pages_per_step=4):
    n_seq, tp, d = q.shape
    n_pages = page_table.shape[1]
    _, n_heads, dh, page = cache_k.shape
    n_steps = n_pages // pages_per_step
    n_rows = n_heads * tp
    blk_keys = pages_per_step * page
    u = jnp.tril(jnp.ones((blk_keys, blk_keys), BF16), k=-1)
    pt_flat = page_table.reshape(-1)

    def page_map(p):
        return lambda s, j, pt: (pt[s * n_pages + (n_steps - 1 - j) * pages_per_step + p], 0, 0, 0)

    seq3 = lambda s, j, pt: (s, 0, 0)
    page_block = (1, n_heads, dh, page)
    blocks = 4 * _nbytes((tp, d), F32) + _nbytes(u.shape, BF16) + 2 * pages_per_step * _nbytes(page_block, F32)
    scratch = (_nbytes((n_heads, 2 * tp, V7X_LANES), BF16) + _nbytes((n_heads, tp, V7X_LANES), F32)
               + _nbytes((n_rows, V7X_LANES), F32))
    grid_spec = pltpu.PrefetchScalarGridSpec(
        num_scalar_prefetch=1,
        grid=(n_seq, n_steps),
        in_specs=[pl.BlockSpec((1, tp, d), seq3), pl.BlockSpec((1, tp, d), seq3), pl.BlockSpec((1, tp, d), seq3),
                  pl.BlockSpec(u.shape, lambda s, j, pt: (0, 0))]
                 + [pl.BlockSpec(page_block, page_map(p)) for p in range(pages_per_step)]
                 + [pl.BlockSpec(page_block, page_map(p)) for p in range(pages_per_step)],
        out_specs=pl.BlockSpec((1, tp, d), seq3),
        scratch_shapes=[pltpu.VMEM((n_heads, tp, dh), BF16), pltpu.VMEM((n_heads, tp, dh), F32),
                        pltpu.VMEM((n_rows, V7X_LANES), F32)],
    )
    return pl.pallas_call(
        functools.partial(_sb_paged_kernel, n_new=n_new, n_heads=n_heads, pages_per_step=pages_per_step),
        grid_spec=grid_spec,
        out_shape=jax.ShapeDtypeStruct((n_seq, tp, d), F32),
        compiler_params=_params(("parallel", "arbitrary"), blocks, scratch, temp_bytes=8 << 20),
        name="sb_paged",
    )(pt_flat, q, k_new, v_new, u, *([cache_k] * pages_per_step), *([cache_v] * pages_per_step))


def _trunk(xf, n_seq, t_pad, t_valid, a_conv, a_ssm, b_conv, mem, past, weights, tm_tok, tm_seq, sb_tq):
    (a_wqkvz, a_wab, a_conv_w, a_alog, a_dtb, a_norm_w, a_w_out, b_w_in, b_conv_w, b_w_out,
     c_w_qkv, c_b_qkv, c_w_out, m_w_q, m_w_o, f_w1, f_w2, ln_g, ln_b, alpha) = weights
    m, d = xf.shape
    depth = ln_g.shape[0]
    xb = xf.astype(BF16)
    zero_bias = jnp.zeros((1, d), F32)
    new_a_conv, new_a_ssm, new_b_conv, new_c_k, new_c_v = [], [], [], [], []
    t_valid_last = t_valid - (t_pad - tm_seq)
    for i in range(depth):
        kind, j = i % 3, i // 3
        g = lambda n: ln_g[i, n][None]
        b = lambda n: ln_b[i, n][None]
        if kind == 0:
            q, k, v, z, gb, cso = a_prep(xf, a_wqkvz[j], a_wab[j], a_conv_w[j], a_alog[j], a_dtb[j], a_conv[j],
                                        n_seq, tm_seq, t_valid_last)
            new_a_conv.append(cso)
            if t_pad % A_CHUNK:
                pad = lambda a: jnp.pad(a.reshape(n_seq, t_pad, -1), ((0, 0), (0, A_CHUNK - t_pad), (0, 0))
                                        ).reshape(n_seq * A_CHUNK, -1)
                o, s_new = gdn(pad(q), pad(k), pad(v), pad(z), pad(gb), a_ssm[j], a_norm_w[j], n_seq, t_valid)
                o = o.reshape(n_seq, A_CHUNK, d)[:, :t_pad].reshape(m, d)
            else:
                o, s_new = gdn(q, k, v, z, gb, a_ssm[j], a_norm_w[j], n_seq, A_CHUNK)
            new_a_ssm.append(s_new)
            xf, xb = mm_ln(o, a_w_out[j], xf, g(0), b(0), alpha, tm_tok)
        elif kind == 1:
            xf, xb, cso = b_mixer(xf, b_w_in[j], b_conv_w[j], b_w_out[j], b_conv[j], g(0), b(0), alpha,
                                  n_seq, tm_seq, t_valid_last)
            new_b_conv.append(cso)
        else:
            dh = d // C_HEADS
            if past is None:
                qs, kf, kb16, vf, vb16 = mm_split(xb, c_w_qkv[j], c_b_qkv[j], [[BF16], [F32, BF16], [F32, BF16]],
                                                  [dh ** -0.5, 1.0, 1.0], tm_tok)
                o = sb_prompt(qs, kb16, vb16, n_seq, sb_tq)
            else:
                qs, kf, vf = mm_split(xb, c_w_qkv[j], c_b_qkv[j], [[F32], [F32], [F32]], [dh ** -0.5, 1.0, 1.0],
                                      tm_tok)
                r3 = lambda a: a.reshape(n_seq, t_pad, d)
                o = sb_paged(r3(qs), r3(kf), r3(vf), past[0][j], past[1][j], past[2], t_valid)
                o = o.reshape(m, d).astype(BF16)
            new_c_k.append(kf)
            new_c_v.append(vf)
            xf, xb = mm_ln(o, c_w_out[j], xf, g(0), b(0), alpha, tm_tok)
        mk, mv = mem[i]
        if past is None:
            xf, xb = memattn_prompt(xb, xf, mk, mv, m_w_q[i], m_w_o[i], g(1), b(1), alpha, n_seq, tm_seq)
        else:
            (qm,) = mm_split(xb, m_w_q[i], zero_bias, [[F32]], [(d // M_HEADS) ** -0.5], tm_tok)
            o = memattn_sample_core(qm.reshape(n_seq, t_pad, d), mk, mv).reshape(m, d).astype(BF16)
            xf, xb = mm_ln(o, m_w_o[i], xf, g(1), b(1), alpha, tm_tok)
        xf, xb = mlp_ln(xb, f_w1[i], f_w2[i], xf, g(2), b(2), alpha, tm_tok)
    return xf, new_a_conv, new_a_ssm, new_b_conv, new_c_k, new_c_v


def kernel(x_prompt, x_sample, state_a_conv, state_a_ssm, state_b_conv, cache_c_k, cache_c_v, cache_mem_k, cache_mem_v, page_table, mem_prompt, a_w_in, a_conv_w, a_a_log, a_dt_bias, a_norm_w, a_w_out, b_w_in, b_conv_w, b_w_out, c_w_qkv, c_b_qkv, c_w_out, m_w_q, m_w_kv, m_w_o, f_w1, f_w2, ln_g, ln_b):
    bp, t_p, d = x_prompt.shape
    bs, t_s, _ = x_sample.shape
    depth = ln_g.shape[0]
    alpha = (2.0 * depth) ** 0.25
    n_a, n_b, n_c = a_w_in.shape[0], b_w_in.shape[0], c_w_qkv.shape[0]
    a_qkv = a_conv_w.shape[2]
    n_mem = mem_prompt.shape[1]
    bf = lambda a: a.astype(BF16)

    a_wqkvz = bf(a_w_in[:, :, :a_qkv + d])
    ab_cols = a_w_in.shape[2] - (a_qkv + d)
    a_wab = bf(jnp.pad(a_w_in[:, :, a_qkv + d:], ((0, 0), (0, 0), (0, V7X_LANES - ab_cols))))
    lane_pad = lambda a: jnp.pad(a, ((0, 0), (0, V7X_LANES - a.shape[1])))[:, None, :]
    weights = (a_wqkvz, a_wab, a_conv_w, lane_pad(a_a_log), lane_pad(a_dt_bias), a_norm_w[:, None, :], bf(a_w_out),
               bf(b_w_in), b_conv_w, bf(b_w_out), bf(c_w_qkv), c_b_qkv.reshape(n_c, 1, 3 * d), bf(c_w_out),
               bf(m_w_q), bf(m_w_o), bf(f_w1), bf(f_w2), ln_g, ln_b, alpha)

    mp = bf(mem_prompt.reshape(bp * n_mem, d))
    zero_kv_bias = jnp.zeros((1, 2 * d), F32)
    w_kv = bf(m_w_kv)
    p_mem, p_mem_k, p_mem_v = [], [], []
    for i in range(depth):
        kf, kb16, vf, vb16 = mm_split(mp, w_kv[i], zero_kv_bias, [[F32, BF16], [F32, BF16]], [1.0, 1.0],
                                      min(PROMPT_TILE, bp * n_mem))
        p_mem.append((kb16, vb16))
        p_mem_k.append(kf)
        p_mem_v.append(vf)
    dm = d // M_HEADS
    p_mem_k = jnp.stack(p_mem_k).reshape(depth, bp, n_mem, M_HEADS, dm)
    p_mem_v = jnp.stack(p_mem_v).reshape(depth, bp, n_mem, M_HEADS, dm)
    yp, pa_conv, pa_ssm, pb_conv, pc_k, pc_v = _trunk(
        x_prompt.reshape(bp * t_p, d), bp, t_p, t_p,
        jnp.zeros((n_a, bp) + state_a_conv.shape[2:], F32), jnp.zeros((n_a, bp) + state_a_ssm.shape[2:], F32),
        jnp.zeros((n_b, bp) + state_b_conv.shape[2:], F32), p_mem, None, weights,
        tm_tok=PROMPT_TILE, tm_seq=min(PROMPT_TILE, t_p), sb_tq=min(PROMPT_TILE, t_p))

    tp = SAMPLE_T_PAD
    xs = jnp.pad(x_sample, ((0, 0), (0, tp - t_s), (0, 0))).reshape(bs * tp, d)
    s_mem = [(cache_mem_k[i].reshape(bs, n_mem, d), cache_mem_v[i].reshape(bs, n_mem, d)) for i in range(depth)]
    past = (jnp.transpose(cache_c_k, (0, 1, 3, 4, 2)), jnp.transpose(cache_c_v, (0, 1, 3, 4, 2)), page_table)
    ys, sa_conv, sa_ssm, sb_conv, sc_k, sc_v = _trunk(
        xs, bs, tp, t_s, state_a_conv, state_a_ssm, state_b_conv, s_mem, past, weights,
        tm_tok=bs * tp, tm_seq=tp, sb_tq=None)

    dh = d // C_HEADS
    unpad = lambda a: a.reshape(bs, tp, -1)[:, :t_s]
    return (yp.reshape(bp, t_p, d), unpad(ys),
            jnp.stack(pa_conv), jnp.stack(pa_ssm), jnp.stack(pb_conv),
            jnp.stack(pc_k).reshape(n_c, bp, t_p, C_HEADS, dh), jnp.stack(pc_v).reshape(n_c, bp, t_p, C_HEADS, dh),
            p_mem_k, p_mem_v,
            jnp.stack(sa_conv), jnp.stack(sa_ssm), jnp.stack(sb_conv),
            jnp.stack([unpad(a) for a in sc_k]).reshape(n_c, bs, t_s, C_HEADS, dh),
            jnp.stack([unpad(a) for a in sc_v]).reshape(n_c, bs, t_s, C_HEADS, dh))
```

```python
import functools
import math

import jax
import jax.numpy as jnp
from jax import lax
from jax.experimental import pallas as pl
from jax.experimental.pallas import tpu as pltpu

F32 = jnp.float32
BF16 = jnp.bfloat16
HIGHEST = lax.Precision.HIGHEST

V7X_LANES = 128
V7X_SUBLANES = 8
V7X_VMEM_BYTES = 64 * 1024 * 1024
VMEM_CAP_BYTES = V7X_VMEM_BYTES - 8 * 1024 * 1024

LN_EPS = 1e-5
NORM_EPS = 1e-6
A_CHUNK = 64
A_HEADS = 8
A_DK = 128
C_HEADS = 16
M_HEADS = 4
SAMPLE_T_PAD = V7X_SUBLANES
PROMPT_TILE = 256

def _vmem_limit(block_bytes, scratch_bytes=0, temp_bytes=0):
    est = 2 * block_bytes + scratch_bytes + temp_bytes + (4 << 20)
    return int(min(max(est, 16 << 20), VMEM_CAP_BYTES))


def _nbytes(shape, dtype):
    return math.prod(shape) * jnp.dtype(dtype).itemsize


def _params(sem, block_bytes, scratch_bytes=0, temp_bytes=0):
    return pltpu.CompilerParams(dimension_semantics=sem,
                                vmem_limit_bytes=_vmem_limit(block_bytes, scratch_bytes, temp_bytes))


def _layer_norm(v, g, b):
    mu = jnp.mean(v, -1, keepdims=True)
    c = v - mu
    var = jnp.mean(c * c, -1, keepdims=True)
    return c * lax.rsqrt(var + LN_EPS) * g + b


def _softplus(x):
    return jnp.maximum(x, 0.0) + jnp.log1p(jnp.exp(-jnp.abs(x)))


def _log_sigmoid_neg(x):
    return jnp.minimum(-x, 0.0) - jnp.log(1.0 + jnp.exp(-jnp.abs(x)))


def _silu(x):
    return x * jax.nn.sigmoid(x)


def _dot(a, b, **kw):
    return jnp.dot(a, b, preferred_element_type=F32, **kw)


def _dot_nt(a, b):
    return lax.dot_general(a, b, (((1,), (1,)), ((), ())), preferred_element_type=F32)


def _bmm(a, b):
    return lax.dot_general(a, b, (((2,), (1,)), ((0,), (0,))), preferred_element_type=F32)


def _bmm_nt(a, b):
    return lax.dot_general(a, b, (((2,), (2,)), ((0,), (0,))), preferred_element_type=F32)


def _mm_split_kernel(x_ref, w_ref, b_ref, *o_refs, d, scales, n_copies):
    x = x_ref[...]
    k = 0
    for j, n_out in enumerate(n_copies):
        cols = slice(j * d, (j + 1) * d)
        y = _dot(x, w_ref[:, cols]) + b_ref[:, cols]
        if scales[j] != 1.0:
            y = y * scales[j]
        for _ in range(n_out):
            o_refs[k][...] = y.astype(o_refs[k].dtype)
            k += 1


def mm_split(x, w, bias, out_dtypes, scales, tm):
    m, kdim = x.shape
    ns = len(out_dtypes)
    d = w.shape[1] // ns
    flat = [dt for group in out_dtypes for dt in group]
    blocks = (_nbytes((tm, kdim), x.dtype) + _nbytes(w.shape, w.dtype) + _nbytes(bias.shape, F32)
              + sum(_nbytes((tm, d), dt) for dt in flat))
    return pl.pallas_call(
        functools.partial(_mm_split_kernel, d=d, scales=tuple(scales), n_copies=tuple(len(g) for g in out_dtypes)),
        grid=(m // tm,),
        in_specs=[pl.BlockSpec((tm, kdim), lambda i: (i, 0)),
                  pl.BlockSpec(w.shape, lambda i: (0, 0)),
                  pl.BlockSpec(bias.shape, lambda i: (0, 0))],
        out_specs=[pl.BlockSpec((tm, d), lambda i: (i, 0)) for _ in flat],
        out_shape=[jax.ShapeDtypeStruct((m, d), dt) for dt in flat],
        compiler_params=_params(("parallel",), blocks, temp_bytes=2 * _nbytes((tm, d), F32)),
        name="mm_split",
    )(x, w, bias)


def _qkv_prompt_kernel(x_ref, w_ref, b_ref, q_ref, k_ref, v_ref, kt_ref, vt_ref, *, q_scale):
    x = x_ref[...]
    d = q_ref.shape[1]
    q = _dot(x, w_ref[:, 0:d]) + b_ref[:, 0:d]
    q_ref[...] = (q * q_scale).astype(q_ref.dtype)
    for j, (o_ref, t_ref) in enumerate(((k_ref, kt_ref), (v_ref, vt_ref)), start=1):
        y = _dot(x, w_ref[:, j * d:(j + 1) * d]) + b_ref[:, j * d:(j + 1) * d]
        o_ref[...] = y.astype(o_ref.dtype)
        t_ref[0] = y.T


def qkv_prompt(x, w, bias, q_scale, n_seq, tm):
    m, kdim = x.shape
    d = w.shape[1] // 3
    t_len = m // n_seq
    n_t = t_len // tm
    row = lambda s, t: (s * n_t + t, 0)
    const = lambda s, t: (0, 0)
    blocks = (_nbytes((tm, kdim), BF16) + _nbytes(w.shape, BF16) + 3 * _nbytes((tm, d), BF16)
              + 2 * _nbytes((d, tm), F32))
    return pl.pallas_call(
        functools.partial(_qkv_prompt_kernel, q_scale=q_scale),
        grid=(n_seq, n_t),
        in_specs=[pl.BlockSpec((tm, kdim), row), pl.BlockSpec(w.shape, const), pl.BlockSpec(bias.shape, const)],
        out_specs=[pl.BlockSpec((tm, d), row)] * 3 + [pl.BlockSpec((1, d, tm), lambda s, t: (s, 0, t))] * 2,
        out_shape=[jax.ShapeDtypeStruct((m, d), BF16)] * 3 + [jax.ShapeDtypeStruct((n_seq, d, t_len), F32)] * 2,
        compiler_params=_params(("parallel", "parallel"), blocks, temp_bytes=4 * _nbytes((tm, d), F32)),
        name="qkv_prompt",
    )(x, w, bias)


def _mm_ln_kernel(x_ref, w_ref, res_ref, g_ref, b_ref, of_ref, ob_ref, *, alpha):
    y = _dot(x_ref[...], w_ref[...])
    v = _layer_norm(alpha * res_ref[...] + y, g_ref[...], b_ref[...])
    of_ref[...] = v
    ob_ref[...] = v.astype(BF16)


def mm_ln(x, w, res, g, b, alpha, tm):
    m, kdim = x.shape
    d = w.shape[1]
    blocks = (_nbytes((tm, kdim), BF16) + _nbytes(w.shape, BF16) + 2 * _nbytes((tm, d), F32)
              + _nbytes((tm, d), BF16))
    row = lambda i: (i, 0)
    const = lambda i: (0, 0)
    return pl.pallas_call(
        functools.partial(_mm_ln_kernel, alpha=alpha),
        grid=(m // tm,),
        in_specs=[pl.BlockSpec((tm, kdim), row), pl.BlockSpec(w.shape, const), pl.BlockSpec((tm, d), row),
                  pl.BlockSpec((1, d), const), pl.BlockSpec((1, d), const)],
        out_specs=[pl.BlockSpec((tm, d), row), pl.BlockSpec((tm, d), row)],
        out_shape=[jax.ShapeDtypeStruct((m, d), F32), jax.ShapeDtypeStruct((m, d), BF16)],
        compiler_params=_params(("parallel",), blocks, temp_bytes=3 * _nbytes((tm, d), F32)),
        name="mm_ln",
    )(x, w, res, g, b)


def _mlp_kernel(x_ref, w1_ref, w2_ref, res_ref, g_ref, b_ref, of_ref, ob_ref, *, alpha, tf):
    x = x_ref[...]
    acc = None
    for c in range(w1_ref.shape[1] // tf):
        cols = slice(c * tf, (c + 1) * tf)
        h = jnp.square(jnp.maximum(_dot(x, w1_ref[:, cols]), 0.0)).astype(BF16)
        part = _dot(h, w2_ref[cols, :])
        acc = part if acc is None else acc + part
    v = _layer_norm(alpha * res_ref[...] + acc, g_ref[...], b_ref[...])
    of_ref[...] = v
    ob_ref[...] = v.astype(BF16)


def mlp_ln(x, w1, w2, res, g, b, alpha, tm, tf=1024):
    m, d = x.shape
    blocks = (_nbytes((tm, d), BF16) + _nbytes(w1.shape, BF16) + _nbytes(w2.shape, BF16)
              + 2 * _nbytes((tm, d), F32) + _nbytes((tm, d), BF16))
    row = lambda i: (i, 0)
    const = lambda i: (0, 0)
    return pl.pallas_call(
        functools.partial(_mlp_kernel, alpha=alpha, tf=tf),
        grid=(m // tm,),
        in_specs=[pl.BlockSpec((tm, d), row), pl.BlockSpec(w1.shape, const), pl.BlockSpec(w2.shape, const),
                  pl.BlockSpec((tm, d), row), pl.BlockSpec((1, d), const), pl.BlockSpec((1, d), const)],
        out_specs=[pl.BlockSpec((tm, d), row), pl.BlockSpec((tm, d), row)],
        out_shape=[jax.ShapeDtypeStruct((m, d), F32), jax.ShapeDtypeStruct((m, d), BF16)],
        compiler_params=_params(("parallel",), blocks, temp_bytes=3 * _nbytes((tm, tf), F32) + 2 * _nbytes((tm, d), F32)),
        name="mlp_ln",
    )(x, w1, w2, res, g, b)


def _memattn_kernel(x_ref, res_ref, mk_ref, mv_ref, wq_ref, wo_ref, g_ref, b_ref, of_ref, ob_ref, *, alpha, n_heads):
    d = x_ref.shape[1]
    dh = d // n_heads
    q = (_dot(x_ref[...], wq_ref[...]) * (dh ** -0.5)).astype(BF16)
    cols = [slice(h * dh, (h + 1) * dh) for h in range(n_heads)]
    s = [_dot_nt(q[:, c], mk_ref[:, c]) for c in cols]
    e = [jnp.exp(sh - jnp.max(sh, -1, keepdims=True)) for sh in s]
    p = [(eh / jnp.sum(eh, -1, keepdims=True)).astype(BF16) for eh in e]
    o = jnp.concatenate([_dot(ph, mv_ref[:, c]).astype(BF16) for ph, c in zip(p, cols)], axis=1)
    v = _layer_norm(alpha * res_ref[...] + _dot(o, wo_ref[...]), g_ref[...], b_ref[...])
    of_ref[...] = v
    ob_ref[...] = v.astype(BF16)


def memattn_prompt(xb, xf, mk, mv, wq, wo, g, b, alpha, n_seq, tm):
    m, d = xb.shape
    n_t = m // n_seq // tm
    n_mem = mk.shape[0] // n_seq
    blocks = (_nbytes((tm, d), BF16) * 2 + _nbytes((tm, d), F32) * 2 + 2 * _nbytes((n_mem, d), BF16)
              + 2 * _nbytes(wq.shape, BF16))
    row = lambda s, t: (s * n_t + t, 0)
    seq = lambda s, t: (s, 0)
    const = lambda s, t: (0, 0)
    return pl.pallas_call(
        functools.partial(_memattn_kernel, alpha=alpha, n_heads=M_HEADS),
        grid=(n_seq, n_t),
        in_specs=[pl.BlockSpec((tm, d), row), pl.BlockSpec((tm, d), row),
                  pl.BlockSpec((n_mem, d), seq), pl.BlockSpec((n_mem, d), seq),
                  pl.BlockSpec(wq.shape, const), pl.BlockSpec(wo.shape, const),
                  pl.BlockSpec((1, d), const), pl.BlockSpec((1, d), const)],
        out_specs=[pl.BlockSpec((tm, d), row), pl.BlockSpec((tm, d), row)],
        out_shape=[jax.ShapeDtypeStruct((m, d), F32), jax.ShapeDtypeStruct((m, d), BF16)],
        compiler_params=_params(("parallel", "parallel"), blocks, temp_bytes=6 * _nbytes((tm, d), F32)),
        name="memattn_prompt",
    )(xb, xf, mk, mv, wq, wo, g, b)


def _memattn_sample_kernel(q_ref, mk_ref, mv_ref, o_ref, *, n_heads):
    q = q_ref[0]
    tp, d = q.shape
    lane_head = lax.broadcasted_iota(jnp.int32, (1, d), 1) // (d // n_heads)
    qexp = jnp.concatenate([jnp.where(lane_head == h, q, 0.0) for h in range(n_heads)], axis=0).astype(BF16)
    s = _dot_nt(qexp, mk_ref[0, 0].astype(BF16))
    e = jnp.exp(s - jnp.max(s, -1, keepdims=True))
    p = e / jnp.sum(e, -1, keepdims=True)
    o_all = _dot(p.astype(BF16), mv_ref[0, 0].astype(BF16))
    out = jnp.zeros_like(q)
    for h in range(n_heads):
        out = out + jnp.where(lane_head == h, o_all[h * tp:(h + 1) * tp, :], 0.0)
    o_ref[0] = out.astype(o_ref.dtype)


def memattn_sample_core(q, mem_k, mem_v, layer):
    n_seq, tp, d = q.shape
    mem_block = (1, 1) + mem_k.shape[2:]
    blocks = 2 * _nbytes((tp, d), F32) + 2 * _nbytes(mem_block, F32)
    mem_map = lambda s: (layer, s, 0, 0)
    return pl.pallas_call(
        functools.partial(_memattn_sample_kernel, n_heads=M_HEADS),
        grid=(n_seq,),
        in_specs=[pl.BlockSpec((1, tp, d), lambda s: (s, 0, 0)),
                  pl.BlockSpec(mem_block, mem_map), pl.BlockSpec(mem_block, mem_map)],
        out_specs=pl.BlockSpec((1, tp, d), lambda s: (s, 0, 0)),
        out_shape=jax.ShapeDtypeStruct((n_seq, tp, d), F32),
        compiler_params=_params(("parallel",), blocks, temp_bytes=2 * _nbytes(mem_block, F32)),
        name="memattn_sample",
    )(q, mem_k, mem_v)


A_HALO_ROW = V7X_SUBLANES


def _a_prep_kernel(x_ref, wqkvz_ref, wab_ref, cw_ref, alog_ref, dtb_ref, cs_ref,
                   q_ref, k_ref, v_ref, z_ref, gb_ref, cso_ref, xp_ref, *, tm, n_t, t_valid_last, n_heads):
    t = pl.program_id(1)
    width = cw_ref.shape[0]
    halo = width - 1
    r0 = A_HALO_ROW
    d = q_ref.shape[1]
    dk = d // n_heads

    @pl.when(t == 0)
    def _():
        xp_ref[r0 - halo:r0, :] = cs_ref[0]

    @pl.when(t > 0)
    def _():
        xp_ref[r0 - halo:r0, :] = xp_ref[r0 + tm - halo:r0 + tm, :]

    x = x_ref[...].astype(BF16)
    for seg, o_ref in enumerate((q_ref, k_ref, v_ref)):
        cols = slice(seg * d, (seg + 1) * d)
        xp_ref[r0:r0 + tm, cols] = _dot(x, wqkvz_ref[:, cols])
        y = xp_ref[r0 - halo:r0 - halo + tm, cols] * cw_ref[0:1, cols]
        for i in range(1, width):
            y = y + xp_ref[r0 - halo + i:r0 - halo + i + tm, cols] * cw_ref[i:i + 1, cols]
        y = _silu(y)
        if seg < 2:
            parts = []
            for h in range(n_heads):
                yh = y[:, h * dk:(h + 1) * dk]
                yh = yh * lax.rsqrt(jnp.sum(yh * yh, -1, keepdims=True) + NORM_EPS)
                parts.append(yh * (dk ** -0.5) if seg == 0 else yh)
            y = jnp.concatenate(parts, axis=1)
        o_ref[...] = y
    z_ref[...] = _dot(x, wqkvz_ref[:, 3 * d:4 * d])
    ab = _dot(x, wab_ref[...])
    g = -jnp.exp(alog_ref[...]) * _softplus(ab + dtb_ref[...])
    lane = lax.broadcasted_iota(jnp.int32, ab.shape, 1)
    gb_ref[...] = jnp.where(lane < n_heads, g, jax.nn.sigmoid(ab))

    @pl.when(t == n_t - 1)
    def _():
        cso_ref[0] = xp_ref[r0 + t_valid_last - halo:r0 + t_valid_last, :]


def a_prep(xf, wqkvz, wab, conv_w, alog, dtb, conv_state, n_seq, tm, t_valid_last):
    m, d = xf.shape
    n_t = m // n_seq // tm
    halo = conv_w.shape[0] - 1
    cq = conv_w.shape[1]
    row = lambda s, t: (s * n_t + t, 0)
    const = lambda s, t: (0, 0)
    seq3 = lambda s, t: (s, 0, 0)
    blocks = (_nbytes((tm, d), F32) + _nbytes(wqkvz.shape, BF16) + _nbytes(wab.shape, BF16)
              + 4 * _nbytes((tm, d), F32) + _nbytes((tm, V7X_LANES), F32) + 2 * _nbytes((V7X_SUBLANES, cq), F32))
    scratch = _nbytes((A_HALO_ROW + tm, cq), F32)
    outs = pl.pallas_call(
        functools.partial(_a_prep_kernel, tm=tm, n_t=n_t, t_valid_last=t_valid_last, n_heads=A_HEADS),
        grid=(n_seq, n_t),
        in_specs=[pl.BlockSpec((tm, d), row), pl.BlockSpec(wqkvz.shape, const), pl.BlockSpec(wab.shape, const),
                  pl.BlockSpec(conv_w.shape, const), pl.BlockSpec(alog.shape, const), pl.BlockSpec(dtb.shape, const),
                  pl.BlockSpec((1, halo, cq), seq3)],
        out_specs=[pl.BlockSpec((tm, d), row)] * 4 + [pl.BlockSpec((tm, V7X_LANES), row),
                                                      pl.BlockSpec((1, halo, cq), seq3)],
        out_shape=[jax.ShapeDtypeStruct((m, d), F32)] * 4 + [jax.ShapeDtypeStruct((m, V7X_LANES), F32),
                                                             jax.ShapeDtypeStruct((n_seq, halo, cq), F32)],
        scratch_shapes=[pltpu.VMEM((A_HALO_ROW + tm, cq), F32)],
        compiler_params=_params(("parallel", "arbitrary"), blocks, scratch, temp_bytes=6 * _nbytes((tm, d), F32)),
        name="a_prep",
    )(xf, wqkvz, wab, conv_w, alog, dtb, conv_state)
    return outs


def _gdn_kernel(q_ref, k_ref, v_ref, z_ref, gb_ref, s0_ref, nw_ref, o_ref, s_ref, *, t_valid, n_heads):
    c = pl.program_id(1)
    ln = q_ref.shape[0]
    dk = q_ref.shape[1] // n_heads

    @pl.when(c == 0)
    def _():
        s_ref[...] = s0_ref[...]

    row = lax.broadcasted_iota(jnp.int32, (ln, ln), 0)
    col = lax.broadcasted_iota(jnp.int32, (ln, ln), 1)
    tri_incl = row >= col
    strict = row > col
    eye = (row == col).astype(F32)
    gb = gb_ref[...]
    row_valid = None
    if t_valid < ln:
        row_valid = lax.broadcasted_iota(jnp.int32, (ln, 1), 0) < t_valid
        gb = jnp.where(row_valid, gb, 0.0)
    g_cum = _dot(tri_incl.astype(F32), gb, precision=HIGHEST)
    g_cum_t = g_cum.T
    g_last = g_cum[ln - 1:ln, :]

    def heads(x):
        return jnp.stack([x[:, h * dk:(h + 1) * dk] for h in range(n_heads)])

    def head_cols(x, first):
        return jnp.stack([x[:, first + h:first + h + 1] for h in range(n_heads)])

    beta = head_cols(gb, n_heads)
    g_col = head_cols(g_cum, 0)
    e_cum = jnp.exp(g_col)
    e_rem = jnp.exp(head_cols(g_last, 0) - g_col)
    e_last = jnp.exp(head_cols(g_last, 0))
    g_row = jnp.stack([g_cum_t[h:h + 1, :] for h in range(n_heads)])
    decay = jnp.where(tri_incl, jnp.exp(g_col - g_row), 0.0)
    q3 = heads(q_ref[...])
    k3 = heads(k_ref[...])
    v3 = heads(v_ref[...])
    if row_valid is not None:
        k3 = jnp.where(row_valid, k3, 0.0)
    k3_b = k3.astype(BF16)
    kb = k3 * beta
    m_mat = jnp.where(strict, _bmm_nt(kb.astype(BF16), k3_b) * decay, 0.0)
    neg = -m_mat
    inv = eye + neg
    pw = neg
    for _ in range(int(math.log2(ln)) - 1):
        pw_b = pw.astype(BF16)
        pw = _bmm(pw_b, pw_b)
        inv = inv + _bmm(inv.astype(BF16), pw.astype(BF16))
    rhs = jnp.concatenate([v3 * beta, kb * e_cum], axis=2)
    sol = _bmm(inv.astype(BF16), rhs.astype(BF16))
    u = sol[:, :, :dk]
    w = sol[:, :, dk:]
    s_all = s_ref[0]
    s_b = s_all.astype(BF16)
    v_new = u - _bmm(w.astype(BF16), s_b)
    v_new_b = v_new.astype(BF16)
    intra = _bmm_nt(q3.astype(BF16), k3_b) * decay
    o = _bmm((q3 * e_cum).astype(BF16), s_b) + _bmm(intra.astype(BF16), v_new_b)
    kd_t = jnp.swapaxes(k3 * e_rem, 1, 2).astype(BF16)
    s_ref[0] = s_all * e_last + _bmm(kd_t, v_new_b)
    o = o * lax.rsqrt(jnp.mean(o * o, -1, keepdims=True) + NORM_EPS) * nw_ref[...] * _silu(heads(z_ref[...]))
    for h in range(n_heads):
        o_ref[:, h * dk:(h + 1) * dk] = o[h].astype(o_ref.dtype)


def gdn(q, k, v, z, gb, s0, norm_w, n_seq, t_valid):
    m, d = q.shape
    ln = A_CHUNK
    n_c = m // n_seq // ln
    row = lambda s, c: (s * n_c + c, 0)
    const = lambda s, c: (0, 0)
    seq4 = lambda s, c: (s, 0, 0, 0)
    sblock = (1,) + s0.shape[1:]
    blocks = (4 * _nbytes((ln, d), F32) + _nbytes((ln, V7X_LANES), F32) + 2 * _nbytes(sblock, F32)
              + _nbytes((ln, d), BF16))
    return pl.pallas_call(
        functools.partial(_gdn_kernel, t_valid=t_valid, n_heads=A_HEADS),
        grid=(n_seq, n_c),
        in_specs=[pl.BlockSpec((ln, d), row)] * 4 + [pl.BlockSpec((ln, V7X_LANES), row),
                                                     pl.BlockSpec(sblock, seq4), pl.BlockSpec(norm_w.shape, const)],
        out_specs=[pl.BlockSpec((ln, d), row), pl.BlockSpec(sblock, seq4)],
        out_shape=[jax.ShapeDtypeStruct((m, d), BF16), jax.ShapeDtypeStruct(s0.shape, F32)],
        compiler_params=_params(("parallel", "arbitrary"), blocks, temp_bytes=8 << 20),
        name="gdn",
    )(q, k, v, z, gb, s0, norm_w)


def _b_mixer_kernel(res_ref, win_ref, cw_ref, wout_ref, cs_ref, g_ref, b_ref,
                    of_ref, ob_ref, cso_ref, xp_ref, *, alpha, tm, n_t, t_valid_last):
    t = pl.program_id(1)
    width = cw_ref.shape[0]
    halo = width - 1
    r0 = A_HALO_ROW
    d = res_ref.shape[1]

    @pl.when(t == 0)
    def _():
        xp_ref[r0 - halo:r0, :] = cs_ref[0]

    @pl.when(t > 0)
    def _():
        xp_ref[r0 - halo:r0, :] = xp_ref[r0 + tm - halo:r0 + tm, :]

    x = res_ref[...].astype(BF16)
    gate_c = _dot(x, win_ref[:, d:2 * d])
    hid = _dot(x, win_ref[:, 2 * d:3 * d])
    xp_ref[r0:r0 + tm, :] = gate_c * hid
    y = xp_ref[r0 - halo:r0 - halo + tm, :] * cw_ref[0:1, :]
    for i in range(1, width):
        y = y + xp_ref[r0 - halo + i:r0 - halo + i + tm, :] * cw_ref[i:i + 1, :]
    gate_b = _dot(x, win_ref[:, 0:d])
    out = _dot((gate_b * y).astype(BF16), wout_ref[...])
    v = _layer_norm(alpha * res_ref[...] + out, g_ref[...], b_ref[...])
    of_ref[...] = v
    ob_ref[...] = v.astype(BF16)

    @pl.when(t == n_t - 1)
    def _():
        cso_ref[0] = xp_ref[r0 + t_valid_last - halo:r0 + t_valid_last, :]


def b_mixer(xf, w_in, conv_w, w_out, conv_state, g, b, alpha, n_seq, tm, t_valid_last):
    m, d = xf.shape
    n_t = m // n_seq // tm
    halo = conv_w.shape[0] - 1
    row = lambda s, t: (s * n_t + t, 0)
    const = lambda s, t: (0, 0)
    seq3 = lambda s, t: (s, 0, 0)
    blocks = (_nbytes((tm, d), BF16) + 2 * _nbytes((tm, d), F32) + _nbytes(w_in.shape, BF16)
              + _nbytes(w_out.shape, BF16) + 2 * _nbytes((V7X_SUBLANES, d), F32))
    scratch = _nbytes((A_HALO_ROW + tm, d), F32)
    return pl.pallas_call(
        functools.partial(_b_mixer_kernel, alpha=alpha, tm=tm, n_t=n_t, t_valid_last=t_valid_last),
        grid=(n_seq, n_t),
        in_specs=[pl.BlockSpec((tm, d), row), pl.BlockSpec(w_in.shape, const),
                  pl.BlockSpec(conv_w.shape, const), pl.BlockSpec(w_out.shape, const),
                  pl.BlockSpec((1, halo, d), seq3), pl.BlockSpec((1, d), const), pl.BlockSpec((1, d), const)],
        out_specs=[pl.BlockSpec((tm, d), row), pl.BlockSpec((tm, d), row), pl.BlockSpec((1, halo, d), seq3)],
        out_shape=[jax.ShapeDtypeStruct((m, d), F32), jax.ShapeDtypeStruct((m, d), BF16),
                   jax.ShapeDtypeStruct((n_seq, halo, d), F32)],
        scratch_shapes=[pltpu.VMEM((A_HALO_ROW + tm, d), F32)],
        compiler_params=_params(("parallel", "arbitrary"), blocks, scratch, temp_bytes=6 * _nbytes((tm, d), F32)),
        name="b_mixer",
    )(xf, w_in, conv_w, w_out, conv_state, g, b)


def _suffix_matrix(n):
    r = lax.broadcasted_iota(jnp.int32, (n, n), 0)
    c = lax.broadcasted_iota(jnp.int32, (n, n), 1)
    return (r > c).astype(BF16)


def _suffix_sum(l, u):
    l_hi = l.astype(BF16)
    l_lo = (l - l_hi.astype(F32)).astype(BF16)
    return _dot(l_hi, u) + _dot(l_lo, u)


def _sb_prompt_kernel(q_ref, k_ref, v_ref, u_ref, o_ref, *, tq, dh):
    i = pl.program_id(2)
    lanes = q_ref.shape[1]
    lane = lax.broadcasted_iota(jnp.int32, (1, lanes), 1)
    n_sub = lanes // dh
    q = q_ref[...]
    qs = [jnp.where(lane // dh == hh, q, jnp.zeros_like(q)) for hh in range(n_sub)]
    r = lax.broadcasted_iota(jnp.int32, (tq, tq), 0)
    c = lax.broadcasted_iota(jnp.int32, (tq, tq), 1)
    causal = c < r
    u = u_ref[...]

    def visit(tiles, state, first_is_diagonal):
        accs, carries = state
        chains = [(t, hh) for t in range(len(tiles)) for hh in range(n_sub)]
        starts = [pl.multiple_of(j * tq, tq) for j in tiles]
        ks = [k_ref[pl.ds(st, tq), :] for st in starts]
        vs = [v_ref[pl.ds(st, tq), :] for st in starts]
        s = {c: _dot_nt(qs[c[1]], ks[c[0]]) for c in chains}
        lsn = {c: _log_sigmoid_neg(s[c]) for c in chains}
        masked = lambda c: first_is_diagonal and c[0] == 0
        l = {c: jnp.where(causal, lsn[c], 0.0) if masked(c) else lsn[c] for c in chains}
        carry_in = {}
        carries = list(carries)
        for t, hh in chains:
            carry_in[(t, hh)] = carries[hh]
            carries[hh] = carries[hh] + jnp.sum(l[(t, hh)], -1, keepdims=True)
        suffix = {c: _suffix_sum(l[c], u) for c in chains}
        w = {c: jnp.exp(s[c] + lsn[c] + suffix[c] + carry_in[c]) for c in chains}
        w = {c: jnp.where(causal, w[c], 0.0) if masked(c) else w[c] for c in chains}
        accs = list(accs)
        for t, hh in chains:
            accs[hh] = accs[hh] + _dot(w[(t, hh)].astype(BF16), vs[t])
        return tuple(accs), tuple(carries)

    init = (tuple(jnp.zeros((tq, lanes), F32) for _ in range(n_sub)),
            tuple(jnp.zeros((tq, 1), F32) for _ in range(n_sub)))
    odd = lax.rem(i, 2)
    state = lax.cond(odd == 1, lambda st: visit([i, i - 1], st, True), lambda st: visit([i], st, True), init)
    first = i - 1 - odd
    state = lax.fori_loop(0, lax.div(i, 2),
                          lambda t, st: visit([first - 2 * t, first - 2 * t - 1], st, False), state)
    accs = state[0]
    out = accs[0]
    for hh in range(1, n_sub):
        out = jnp.where(lane // dh == hh, accs[hh], out)
    o_ref[...] = out.astype(o_ref.dtype)


def sb_prompt(q, k, v, n_seq, tq):
    m, d = q.shape
    t_len = m // n_seq
    n_q = t_len // tq
    dh = d // C_HEADS
    lanes = V7X_LANES
    u = jnp.tril(jnp.ones((tq, tq), BF16), k=-1)
    blocks = 2 * _nbytes((tq, lanes), BF16) + 2 * _nbytes((t_len, lanes), BF16) + _nbytes((tq, tq), BF16)
    return pl.pallas_call(
        functools.partial(_sb_prompt_kernel, tq=tq, dh=dh),
        grid=(n_seq, d // lanes, n_q),
        in_specs=[pl.BlockSpec((tq, lanes), lambda s, p, i: (s * n_q + i, p)),
                  pl.BlockSpec((t_len, lanes), lambda s, p, i: (s, p)),
                  pl.BlockSpec((t_len, lanes), lambda s, p, i: (s, p)),
                  pl.BlockSpec((tq, tq), lambda s, p, i: (0, 0))],
        out_specs=pl.BlockSpec((tq, lanes), lambda s, p, i: (s * n_q + i, p)),
        out_shape=jax.ShapeDtypeStruct((m, d), BF16),
        compiler_params=_params(("parallel", "parallel", "arbitrary"), blocks, temp_bytes=16 * _nbytes((tq, tq), F32)),
        name="sb_prompt",
    )(q, k, v, u)


def _sb_paged_kernel(pt_ref, q_ref, kn_ref, vn_ref, u_ref, *refs, n_new, n_heads, pages_per_step):
    del pt_ref
    k_refs = refs[:pages_per_step]
    v_refs = refs[pages_per_step:2 * pages_per_step]
    o_ref = refs[2 * pages_per_step]
    q3_ref, acc_ref, carry_ref = refs[2 * pages_per_step + 1:]
    j = pl.program_id(1)
    tp, d = q_ref.shape[1:]
    dh = d // n_heads
    page = k_refs[0].shape[3]
    n_rows = n_heads * tp

    def heads(x):
        return jnp.stack([x[:, h * dh:(h + 1) * dh] for h in range(n_heads)])

    @pl.when(j == 0)
    def _():
        q3 = heads(q_ref[0])
        q3_ref[...] = q3.astype(BF16)
        kn3 = heads(kn_ref[0])
        vn3 = heads(vn_ref[0])
        q_pos = lax.broadcasted_iota(jnp.int32, (1, tp, 1), 1)
        carry = jnp.zeros((n_heads, tp, 1), F32)
        acc = jnp.zeros((n_heads, tp, dh), F32)
        for t_key in reversed(range(n_new)):
            valid = t_key < q_pos
            s = jnp.sum(q3 * kn3[:, t_key:t_key + 1, :], -1, keepdims=True)
            lsn = _log_sigmoid_neg(s)
            w = jnp.where(valid, jnp.exp(s + lsn + carry), 0.0)
            acc = acc + w * vn3[:, t_key:t_key + 1, :]
            carry = carry + jnp.where(valid, lsn, 0.0)
        acc_ref[...] = acc
        carry_ref[...] = jnp.broadcast_to(carry.reshape(n_rows, 1), carry_ref.shape)

    q3_b = q3_ref[...]
    u = u_ref[...]
    per_block = u.shape[0] // page
    blocks = [range(b0, b0 + per_block) for b0 in reversed(range(0, pages_per_step, per_block))]
    cat = lambda refs, blk: jnp.concatenate([refs[p][0] for p in blk], axis=2).astype(BF16)
    s = [_bmm(q3_b, cat(k_refs, blk)).reshape(n_rows, u.shape[0]) for blk in blocks]
    l = [_log_sigmoid_neg(sb) for sb in s]
    carry = carry_ref[:, 0:1]
    carry_in = []
    for lb in l:
        carry_in.append(carry)
        carry = carry + jnp.sum(lb, -1, keepdims=True)
    suffix = [_suffix_sum(lb, u) for lb in l]
    w = [jnp.exp(sb + lb + fb + cb).reshape(n_heads, tp, u.shape[0]).astype(BF16)
         for sb, lb, fb, cb in zip(s, l, suffix, carry_in)]
    acc = acc_ref[...]
    for wb, blk in zip(w, blocks):
        acc = acc + _bmm_nt(wb, cat(v_refs, blk))
    acc_ref[...] = acc
    carry_ref[...] = jnp.broadcast_to(carry, carry_ref.shape)

    @pl.when(j == pl.num_programs(1) - 1)
    def _():
        res = acc_ref[...]
        out = jnp.concatenate([res[h] for h in range(n_heads)], axis=1)
        is_new = lax.broadcasted_iota(jnp.int32, (tp, 1), 0) < n_new
        o_ref[0] = jnp.where(is_new, out, 0.0).astype(o_ref.dtype)


def sb_paged(q, k_new, v_new, cache_k, cache_v, page_table, n_new, pages_per_step=8, pages_per_block=4):
    n_seq, tp, d = q.shape
    n_pages = page_table.shape[1]
    _, n_heads, dh, page = cache_k.shape
    assert n_pages % pages_per_step == 0 and pages_per_step % pages_per_block == 0
    n_steps = n_pages // pages_per_step
    n_rows = n_heads * tp
    blk_keys = pages_per_block * page
    u = jnp.tril(jnp.ones((blk_keys, blk_keys), BF16), k=-1)
    pt_flat = page_table.reshape(-1)

    def page_map(p):
        return lambda s, j, pt: (pt[s * n_pages + (n_steps - 1 - j) * pages_per_step + p], 0, 0, 0)

    seq3 = lambda s, j, pt: (s, 0, 0)
    page_block = (1, n_heads, dh, page)
    blocks = 4 * _nbytes((tp, d), F32) + _nbytes(u.shape, BF16) + 2 * pages_per_step * _nbytes(page_block, F32)
    scratch = (_nbytes((n_heads, 2 * tp, V7X_LANES), BF16) + _nbytes((n_heads, tp, V7X_LANES), F32)
               + _nbytes((n_rows, V7X_LANES), F32))
    grid_spec = pltpu.PrefetchScalarGridSpec(
        num_scalar_prefetch=1,
        grid=(n_seq, n_steps),
        in_specs=[pl.BlockSpec((1, tp, d), seq3), pl.BlockSpec((1, tp, d), seq3), pl.BlockSpec((1, tp, d), seq3),
                  pl.BlockSpec(u.shape, lambda s, j, pt: (0, 0))]
                 + [pl.BlockSpec(page_block, page_map(p)) for p in range(pages_per_step)]
                 + [pl.BlockSpec(page_block, page_map(p)) for p in range(pages_per_step)],
        out_specs=pl.BlockSpec((1, tp, d), seq3),
        scratch_shapes=[pltpu.VMEM((n_heads, tp, dh), BF16), pltpu.VMEM((n_heads, tp, dh), F32),
                        pltpu.VMEM((n_rows, V7X_LANES), F32)],
    )
    return pl.pallas_call(
        functools.partial(_sb_paged_kernel, n_new=n_new, n_heads=n_heads, pages_per_step=pages_per_step),
        grid_spec=grid_spec,
        out_shape=jax.ShapeDtypeStruct((n_seq, tp, d), F32),
        compiler_params=_params(("parallel", "arbitrary"), blocks, scratch, temp_bytes=8 << 20),
        name="sb_paged",
    )(pt_flat, q, k_new, v_new, u, *([cache_k] * pages_per_step), *([cache_v] * pages_per_step))


def _trunk(xf, n_seq, t_pad, t_valid, a_conv, a_ssm, b_conv, mem, past, weights, tm_tok, tm_seq, sb_tq):
    (a_wqkvz, a_wab, a_conv_w, a_alog, a_dtb, a_norm_w, a_w_out, b_w_in, b_conv_w, b_w_out,
     c_w_qkv, c_b_qkv, c_w_out, m_w_q, m_w_o, f_w1, f_w2, ln_g, ln_b, alpha) = weights
    m, d = xf.shape
    depth = ln_g.shape[0]
    xb = xf.astype(BF16)
    zero_bias = jnp.zeros((1, d), F32)
    new_a_conv, new_a_ssm, new_b_conv, new_c_k, new_c_v = [], [], [], [], []
    t_valid_last = t_valid - (t_pad - tm_seq)
    for i in range(depth):
        kind, j = i % 3, i // 3
        g = lambda n: ln_g[i, n][None]
        b = lambda n: ln_b[i, n][None]
        if kind == 0:
            q, k, v, z, gb, cso = a_prep(xf, a_wqkvz[j], a_wab[j], a_conv_w[j], a_alog[j], a_dtb[j], a_conv[j],
                                        n_seq, tm_seq, t_valid_last)
            new_a_conv.append(cso)
            if t_pad % A_CHUNK:
                pad = lambda a: jnp.pad(a.reshape(n_seq, t_pad, -1), ((0, 0), (0, A_CHUNK - t_pad), (0, 0))
                                        ).reshape(n_seq * A_CHUNK, -1)
                o, s_new = gdn(pad(q), pad(k), pad(v), pad(z), pad(gb), a_ssm[j], a_norm_w[j], n_seq, t_valid)
                o = o.reshape(n_seq, A_CHUNK, d)[:, :t_pad].reshape(m, d)
            else:
                o, s_new = gdn(q, k, v, z, gb, a_ssm[j], a_norm_w[j], n_seq, A_CHUNK)
            new_a_ssm.append(s_new)
            xf, xb = mm_ln(o, a_w_out[j], xf, g(0), b(0), alpha, tm_tok)
        elif kind == 1:
            xf, xb, cso = b_mixer(xf, b_w_in[j], b_conv_w[j], b_w_out[j], b_conv[j], g(0), b(0), alpha,
                                  n_seq, tm_seq, t_valid_last)
            new_b_conv.append(cso)
        else:
            dh = d // C_HEADS
            if past is None:
                qs, kb16, vb16, kf, vf = qkv_prompt(xb, c_w_qkv[j], c_b_qkv[j], dh ** -0.5, n_seq, tm_seq)
                o = sb_prompt(qs, kb16, vb16, n_seq, sb_tq)
            else:
                qs, kf, vf = mm_split(xb, c_w_qkv[j], c_b_qkv[j], [[F32], [F32], [F32]], [dh ** -0.5, 1.0, 1.0],
                                      tm_tok)
                r3 = lambda a: a.reshape(n_seq, t_pad, d)
                o = sb_paged(r3(qs), r3(kf), r3(vf), past[0][j], past[1][j], past[2], t_valid)
                o = o.reshape(m, d).astype(BF16)
            new_c_k.append(kf)
            new_c_v.append(vf)
            xf, xb = mm_ln(o, c_w_out[j], xf, g(0), b(0), alpha, tm_tok)
        if past is None:
            mk, mv = mem[i]
            xf, xb = memattn_prompt(xb, xf, mk, mv, m_w_q[i], m_w_o[i], g(1), b(1), alpha, n_seq, tm_seq)
        else:
            (qm,) = mm_split(xb, m_w_q[i], zero_bias, [[F32]], [(d // M_HEADS) ** -0.5], tm_tok)
            o = memattn_sample_core(qm.reshape(n_seq, t_pad, d), mem[0], mem[1], i).reshape(m, d).astype(BF16)
            xf, xb = mm_ln(o, m_w_o[i], xf, g(1), b(1), alpha, tm_tok)
        xf, xb = mlp_ln(xb, f_w1[i], f_w2[i], xf, g(2), b(2), alpha, tm_tok)
    return xf, new_a_conv, new_a_ssm, new_b_conv, new_c_k, new_c_v


def kernel(x_prompt, x_sample, state_a_conv, state_a_ssm, state_b_conv, cache_c_k, cache_c_v, cache_mem_k, cache_mem_v, page_table, mem_prompt, a_w_in, a_conv_w, a_a_log, a_dt_bias, a_norm_w, a_w_out, b_w_in, b_conv_w, b_w_out, c_w_qkv, c_b_qkv, c_w_out, m_w_q, m_w_kv, m_w_o, f_w1, f_w2, ln_g, ln_b):
    bp, t_p, d = x_prompt.shape
    bs, t_s, _ = x_sample.shape
    depth = ln_g.shape[0]
    alpha = (2.0 * depth) ** 0.25
    n_a, n_b, n_c = a_w_in.shape[0], b_w_in.shape[0], c_w_qkv.shape[0]
    a_qkv = a_conv_w.shape[2]
    n_mem = mem_prompt.shape[1]
    bf = lambda a: a.astype(BF16)

    a_wqkvz = bf(a_w_in[:, :, :a_qkv + d])
    ab_cols = a_w_in.shape[2] - (a_qkv + d)
    a_wab = bf(jnp.pad(a_w_in[:, :, a_qkv + d:], ((0, 0), (0, 0), (0, V7X_LANES - ab_cols))))
    lane_pad = lambda a: jnp.pad(a, ((0, 0), (0, V7X_LANES - a.shape[1])))[:, None, :]
    weights = (a_wqkvz, a_wab, a_conv_w, lane_pad(a_a_log), lane_pad(a_dt_bias), a_norm_w[:, None, :], bf(a_w_out),
               bf(b_w_in), b_conv_w, bf(b_w_out), bf(c_w_qkv), c_b_qkv.reshape(n_c, 1, 3 * d), bf(c_w_out),
               bf(m_w_q), bf(m_w_o), bf(f_w1), bf(f_w2), ln_g, ln_b, alpha)

    mp = bf(mem_prompt.reshape(bp * n_mem, d))
    zero_kv_bias = jnp.zeros((1, 2 * d), F32)
    w_kv = bf(m_w_kv)
    p_mem, p_mem_k, p_mem_v = [], [], []
    for i in range(depth):
        kf, kb16, vf, vb16 = mm_split(mp, w_kv[i], zero_kv_bias, [[F32, BF16], [F32, BF16]], [1.0, 1.0],
                                      min(PROMPT_TILE, bp * n_mem))
        p_mem.append((kb16, vb16))
        p_mem_k.append(kf)
        p_mem_v.append(vf)
    dm = d // M_HEADS
    p_mem_k = jnp.stack(p_mem_k).reshape(depth, bp, n_mem, M_HEADS, dm)
    p_mem_v = jnp.stack(p_mem_v).reshape(depth, bp, n_mem, M_HEADS, dm)
    yp, pa_conv, pa_ssm, pb_conv, pc_k, pc_v = _trunk(
        x_prompt.reshape(bp * t_p, d), bp, t_p, t_p,
        jnp.zeros((n_a, bp) + state_a_conv.shape[2:], F32), jnp.zeros((n_a, bp) + state_a_ssm.shape[2:], F32),
        jnp.zeros((n_b, bp) + state_b_conv.shape[2:], F32), p_mem, None, weights,
        tm_tok=PROMPT_TILE, tm_seq=min(PROMPT_TILE, t_p), sb_tq=min(PROMPT_TILE, t_p))

    tp = SAMPLE_T_PAD
    xs = jnp.pad(x_sample, ((0, 0), (0, tp - t_s), (0, 0))).reshape(bs * tp, d)
    s_mem = (cache_mem_k.reshape(depth, bs, n_mem, d), cache_mem_v.reshape(depth, bs, n_mem, d))
    past = (jnp.transpose(cache_c_k, (0, 1, 3, 4, 2)), jnp.transpose(cache_c_v, (0, 1, 3, 4, 2)), page_table)
    ys, sa_conv, sa_ssm, sb_conv, sc_k, sc_v = _trunk(
        xs, bs, tp, t_s, state_a_conv, state_a_ssm, state_b_conv, s_mem, past, weights,
        tm_tok=bs * tp, tm_seq=tp, sb_tq=None)

    dh = d // C_HEADS
    unpad = lambda a: a.reshape(bs, tp, -1)[:, :t_s]
    from_t = lambda a: jnp.transpose(jnp.stack(a).reshape(n_c, bp, C_HEADS, dh, t_p), (0, 1, 4, 2, 3))
    return (yp.reshape(bp, t_p, d), unpad(ys),
            jnp.stack(pa_conv), jnp.stack(pa_ssm), jnp.stack(pb_conv),
            from_t(pc_k), from_t(pc_v),
            p_mem_k, p_mem_v,
            jnp.stack(sa_conv), jnp.stack(sa_ssm), jnp.stack(sb_conv),
            jnp.stack([unpad(a) for a in sc_k]).reshape(n_c, bs, t_s, C_HEADS, dh),
            jnp.stack([unpad(a) for a in sc_v]).reshape(n_c, bs, t_s, C_HEADS, dh))
```

```python
import functools
import math

import jax
import jax.numpy as jnp
from jax import lax
from jax.experimental import pallas as pl
from jax.experimental.pallas import tpu as pltpu

F32 = jnp.float32
BF16 = jnp.bfloat16
HIGHEST = lax.Precision.HIGHEST

V7X_LANES = 128
V7X_SUBLANES = 8
V7X_VMEM_BYTES = 64 * 1024 * 1024
VMEM_CAP_BYTES = V7X_VMEM_BYTES - 8 * 1024 * 1024

LN_EPS = 1e-5
NORM_EPS = 1e-6
A_CHUNK = 64
A_HEADS = 8
A_DK = 128
C_HEADS = 16
M_HEADS = 4
SAMPLE_T_PAD = V7X_SUBLANES
PROMPT_TILE = 256
SB_TILES_PER_ITER = 4
GDN_SEQS_PER_STEP = 4

def _vmem_limit(block_bytes, scratch_bytes=0, temp_bytes=0):
    est = 2 * block_bytes + scratch_bytes + temp_bytes + (4 << 20)
    return int(min(max(est, 16 << 20), VMEM_CAP_BYTES))


def _nbytes(shape, dtype):
    return math.prod(shape) * jnp.dtype(dtype).itemsize


def _params(sem, block_bytes, scratch_bytes=0, temp_bytes=0):
    return pltpu.CompilerParams(dimension_semantics=sem,
                                vmem_limit_bytes=_vmem_limit(block_bytes, scratch_bytes, temp_bytes))


def _layer_norm(v, g, b):
    mu = jnp.mean(v, -1, keepdims=True)
    c = v - mu
    var = jnp.mean(c * c, -1, keepdims=True)
    return c * lax.rsqrt(var + LN_EPS) * g + b


def _softplus(x):
    return jnp.maximum(x, 0.0) + jnp.log1p(jnp.exp(-jnp.abs(x)))


def _log_sigmoid_neg(x):
    return jnp.minimum(-x, 0.0) - jnp.log(1.0 + jnp.exp(-jnp.abs(x)))


def _silu(x):
    return x * jax.nn.sigmoid(x)


def _dot(a, b, **kw):
    return jnp.dot(a, b, preferred_element_type=F32, **kw)


def _dot_nt(a, b):
    return lax.dot_general(a, b, (((1,), (1,)), ((), ())), preferred_element_type=F32)


def _bmm(a, b):
    return lax.dot_general(a, b, (((2,), (1,)), ((0,), (0,))), preferred_element_type=F32)


def _bmm_nt(a, b):
    return lax.dot_general(a, b, (((2,), (2,)), ((0,), (0,))), preferred_element_type=F32)


def _mm_split_kernel(x_ref, w_ref, b_ref, *o_refs, d, scales, n_copies):
    x = x_ref[...]
    k = 0
    for j, n_out in enumerate(n_copies):
        cols = slice(j * d, (j + 1) * d)
        y = _dot(x, w_ref[:, cols]) + b_ref[:, cols]
        if scales[j] != 1.0:
            y = y * scales[j]
        for _ in range(n_out):
            o_refs[k][...] = y.astype(o_refs[k].dtype)
            k += 1


def mm_split(x, w, bias, out_dtypes, scales, tm):
    m, kdim = x.shape
    ns = len(out_dtypes)
    d = w.shape[1] // ns
    flat = [dt for group in out_dtypes for dt in group]
    blocks = (_nbytes((tm, kdim), x.dtype) + _nbytes(w.shape, w.dtype) + _nbytes(bias.shape, F32)
              + sum(_nbytes((tm, d), dt) for dt in flat))
    return pl.pallas_call(
        functools.partial(_mm_split_kernel, d=d, scales=tuple(scales), n_copies=tuple(len(g) for g in out_dtypes)),
        grid=(m // tm,),
        in_specs=[pl.BlockSpec((tm, kdim), lambda i: (i, 0)),
                  pl.BlockSpec(w.shape, lambda i: (0, 0)),
                  pl.BlockSpec(bias.shape, lambda i: (0, 0))],
        out_specs=[pl.BlockSpec((tm, d), lambda i: (i, 0)) for _ in flat],
        out_shape=[jax.ShapeDtypeStruct((m, d), dt) for dt in flat],
        compiler_params=_params(("parallel",), blocks, temp_bytes=2 * _nbytes((tm, d), F32)),
        name="mm_split",
    )(x, w, bias)


def _memkv_kernel(x_ref, w_ref, kb_ref, vb_ref, kr_ref, vr_ref, *, n_heads):
    x = x_ref[...]
    d = kb_ref.shape[2]
    dh = d // n_heads
    n_tiles = dh // V7X_LANES
    group = n_heads * n_tiles
    n_mem = x.shape[0]
    for j, (b_ref, r_ref) in enumerate(((kb_ref, kr_ref), (vb_ref, vr_ref))):
        y = _dot(x, w_ref[0, :, j * d:(j + 1) * d])
        b_ref[0] = y.astype(b_ref.dtype)
        for h in range(n_heads):
            for lt in range(n_tiles):
                lo = h * dh + lt * V7X_LANES
                r_ref[0, 0, pl.ds(lt * n_heads + h, n_mem, stride=group), :] = y[:, lo:lo + V7X_LANES]


def memkv_proj(x, w_kv, n_seq):
    m, d = x.shape
    depth = w_kv.shape[0]
    n_mem = m // n_seq
    rows = n_mem * d // V7X_LANES
    blocks = (_nbytes((n_mem, d), BF16) + _nbytes((d, 2 * d), BF16) + 2 * _nbytes((n_mem, d), BF16)
              + 2 * _nbytes((n_mem, d), F32))
    return pl.pallas_call(
        functools.partial(_memkv_kernel, n_heads=M_HEADS),
        grid=(depth, n_seq),
        in_specs=[pl.BlockSpec((n_mem, d), lambda l, s: (s, 0)),
                  pl.BlockSpec((1, d, 2 * d), lambda l, s: (l, 0, 0))],
        out_specs=[pl.BlockSpec((1, n_mem, d), lambda l, s: (l, s, 0))] * 2
                  + [pl.BlockSpec((1, 1, rows, V7X_LANES), lambda l, s: (l, s, 0, 0))] * 2,
        out_shape=[jax.ShapeDtypeStruct((depth, m, d), BF16)] * 2
                  + [jax.ShapeDtypeStruct((depth, n_seq, rows, V7X_LANES), F32)] * 2,
        compiler_params=_params(("parallel", "parallel"), blocks, temp_bytes=4 * _nbytes((n_mem, d), F32)),
        name="memkv_proj",
    )(x, w_kv)


def _qkv_prompt_kernel(x_ref, w_ref, b_ref, q_ref, k_ref, v_ref, kt_ref, vt_ref, *, q_scale):
    x = x_ref[...]
    d = q_ref.shape[1]
    q = _dot(x, w_ref[:, 0:d]) + b_ref[:, 0:d]
    q_ref[...] = (q * q_scale).astype(q_ref.dtype)
    for j, (o_ref, t_ref) in enumerate(((k_ref, kt_ref), (v_ref, vt_ref)), start=1):
        y = _dot(x, w_ref[:, j * d:(j + 1) * d]) + b_ref[:, j * d:(j + 1) * d]
        o_ref[...] = y.astype(o_ref.dtype)
        t_ref[0] = y.T


def qkv_prompt(x, w, bias, q_scale, n_seq, tm):
    m, kdim = x.shape
    d = w.shape[1] // 3
    t_len = m // n_seq
    n_t = t_len // tm
    row = lambda s, t: (s * n_t + t, 0)
    const = lambda s, t: (0, 0)
    blocks = (_nbytes((tm, kdim), BF16) + _nbytes(w.shape, BF16) + 3 * _nbytes((tm, d), BF16)
              + 2 * _nbytes((d, tm), F32))
    return pl.pallas_call(
        functools.partial(_qkv_prompt_kernel, q_scale=q_scale),
        grid=(n_seq, n_t),
        in_specs=[pl.BlockSpec((tm, kdim), row), pl.BlockSpec(w.shape, const), pl.BlockSpec(bias.shape, const)],
        out_specs=[pl.BlockSpec((tm, d), row)] * 3 + [pl.BlockSpec((1, d, tm), lambda s, t: (s, 0, t))] * 2,
        out_shape=[jax.ShapeDtypeStruct((m, d), BF16)] * 3 + [jax.ShapeDtypeStruct((n_seq, d, t_len), F32)] * 2,
        compiler_params=_params(("parallel", "parallel"), blocks, temp_bytes=4 * _nbytes((tm, d), F32)),
        name="qkv_prompt",
    )(x, w, bias)


def _mm_ln_kernel(x_ref, w_ref, res_ref, g_ref, b_ref, of_ref, ob_ref, *, alpha):
    y = _dot(x_ref[...], w_ref[...])
    v = _layer_norm(alpha * res_ref[...] + y, g_ref[...], b_ref[...])
    of_ref[...] = v
    ob_ref[...] = v.astype(BF16)


def mm_ln(x, w, res, g, b, alpha, tm):
    m, kdim = x.shape
    d = w.shape[1]
    blocks = (_nbytes((tm, kdim), BF16) + _nbytes(w.shape, BF16) + 2 * _nbytes((tm, d), F32)
              + _nbytes((tm, d), BF16))
    row = lambda i: (i, 0)
    const = lambda i: (0, 0)
    return pl.pallas_call(
        functools.partial(_mm_ln_kernel, alpha=alpha),
        grid=(m // tm,),
        in_specs=[pl.BlockSpec((tm, kdim), row), pl.BlockSpec(w.shape, const), pl.BlockSpec((tm, d), row),
                  pl.BlockSpec((1, d), const), pl.BlockSpec((1, d), const)],
        out_specs=[pl.BlockSpec((tm, d), row), pl.BlockSpec((tm, d), row)],
        out_shape=[jax.ShapeDtypeStruct((m, d), F32), jax.ShapeDtypeStruct((m, d), BF16)],
        compiler_params=_params(("parallel",), blocks, temp_bytes=3 * _nbytes((tm, d), F32)),
        name="mm_ln",
    )(x, w, res, g, b)


def _mlp_kernel(x_ref, w1_ref, w2_ref, res_ref, g_ref, b_ref, of_ref, ob_ref, *, alpha, tf):
    x = x_ref[...]
    acc = None
    for c in range(w1_ref.shape[1] // tf):
        cols = slice(c * tf, (c + 1) * tf)
        h = jnp.square(jnp.maximum(_dot(x, w1_ref[:, cols]), 0.0)).astype(BF16)
        part = _dot(h, w2_ref[cols, :])
        acc = part if acc is None else acc + part
    v = _layer_norm(alpha * res_ref[...] + acc, g_ref[...], b_ref[...])
    of_ref[...] = v
    ob_ref[...] = v.astype(BF16)


def mlp_ln(x, w1, w2, res, g, b, alpha, tm, tf=1024):
    m, d = x.shape
    blocks = (_nbytes((tm, d), BF16) + _nbytes(w1.shape, BF16) + _nbytes(w2.shape, BF16)
              + 2 * _nbytes((tm, d), F32) + _nbytes((tm, d), BF16))
    row = lambda i: (i, 0)
    const = lambda i: (0, 0)
    return pl.pallas_call(
        functools.partial(_mlp_kernel, alpha=alpha, tf=tf),
        grid=(m // tm,),
        in_specs=[pl.BlockSpec((tm, d), row), pl.BlockSpec(w1.shape, const), pl.BlockSpec(w2.shape, const),
                  pl.BlockSpec((tm, d), row), pl.BlockSpec((1, d), const), pl.BlockSpec((1, d), const)],
        out_specs=[pl.BlockSpec((tm, d), row), pl.BlockSpec((tm, d), row)],
        out_shape=[jax.ShapeDtypeStruct((m, d), F32), jax.ShapeDtypeStruct((m, d), BF16)],
        compiler_params=_params(("parallel",), blocks, temp_bytes=3 * _nbytes((tm, tf), F32) + 2 * _nbytes((tm, d), F32)),
        name="mlp_ln",
    )(x, w1, w2, res, g, b)


def _memattn_kernel(x_ref, res_ref, mk_ref, mv_ref, wq_ref, wo_ref, g_ref, b_ref, of_ref, ob_ref, *, alpha, n_heads):
    d = x_ref.shape[1]
    dh = d // n_heads
    q = (_dot(x_ref[...], wq_ref[...]) * (dh ** -0.5)).astype(BF16)
    cols = [slice(h * dh, (h + 1) * dh) for h in range(n_heads)]
    s = [_dot_nt(q[:, c], mk_ref[:, c]) for c in cols]
    e = [jnp.exp(sh - jnp.max(sh, -1, keepdims=True)) for sh in s]
    p = [(eh / jnp.sum(eh, -1, keepdims=True)).astype(BF16) for eh in e]
    o = jnp.concatenate([_dot(ph, mv_ref[:, c]).astype(BF16) for ph, c in zip(p, cols)], axis=1)
    v = _layer_norm(alpha * res_ref[...] + _dot(o, wo_ref[...]), g_ref[...], b_ref[...])
    of_ref[...] = v
    ob_ref[...] = v.astype(BF16)


def memattn_prompt(xb, xf, mk, mv, wq, wo, g, b, alpha, n_seq, tm):
    m, d = xb.shape
    n_t = m // n_seq // tm
    n_mem = mk.shape[0] // n_seq
    blocks = (_nbytes((tm, d), BF16) * 2 + _nbytes((tm, d), F32) * 2 + 2 * _nbytes((n_mem, d), BF16)
              + 2 * _nbytes(wq.shape, BF16))
    row = lambda s, t: (s * n_t + t, 0)
    seq = lambda s, t: (s, 0)
    const = lambda s, t: (0, 0)
    return pl.pallas_call(
        functools.partial(_memattn_kernel, alpha=alpha, n_heads=M_HEADS),
        grid=(n_seq, n_t),
        in_specs=[pl.BlockSpec((tm, d), row), pl.BlockSpec((tm, d), row),
                  pl.BlockSpec((n_mem, d), seq), pl.BlockSpec((n_mem, d), seq),
                  pl.BlockSpec(wq.shape, const), pl.BlockSpec(wo.shape, const),
                  pl.BlockSpec((1, d), const), pl.BlockSpec((1, d), const)],
        out_specs=[pl.BlockSpec((tm, d), row), pl.BlockSpec((tm, d), row)],
        out_shape=[jax.ShapeDtypeStruct((m, d), F32), jax.ShapeDtypeStruct((m, d), BF16)],
        compiler_params=_params(("parallel", "parallel"), blocks, temp_bytes=6 * _nbytes((tm, d), F32)),
        name="memattn_prompt",
    )(xb, xf, mk, mv, wq, wo, g, b)


def _memattn_sample_kernel(q_ref, mk_ref, mv_ref, o_ref, *, n_heads):
    q = q_ref[0].astype(BF16)
    d = q.shape[1]
    dh = d // n_heads
    n_tiles = dh // V7X_LANES
    group = n_heads * n_tiles
    n_mem = mk_ref.shape[2] // group
    slab = lambda ref, h, lt: ref[0, 0, pl.ds(lt * n_heads + h, n_mem, stride=group), :].astype(BF16)
    lanes = lambda h, lt: slice(h * dh + lt * V7X_LANES, h * dh + (lt + 1) * V7X_LANES)
    s = [sum(_dot_nt(q[:, lanes(h, lt)], slab(mk_ref, h, lt)) for lt in range(n_tiles)) for h in range(n_heads)]
    e = [jnp.exp(sh - jnp.max(sh, -1, keepdims=True)) for sh in s]
    p = [(eh / jnp.sum(eh, -1, keepdims=True)).astype(BF16) for eh in e]
    o = [_dot(p[h], slab(mv_ref, h, lt)) for h in range(n_heads) for lt in range(n_tiles)]
    o_ref[0] = jnp.concatenate(o, axis=1).astype(o_ref.dtype)


def memattn_sample_core(q, mem_k, mem_v, layer):
    n_seq, tp, d = q.shape
    mem_block = (1, 1) + mem_k.shape[2:]
    blocks = 2 * _nbytes((tp, d), F32) + 2 * _nbytes(mem_block, F32)
    mem_map = lambda s: (layer, s, 0, 0)
    return pl.pallas_call(
        functools.partial(_memattn_sample_kernel, n_heads=M_HEADS),
        grid=(n_seq,),
        in_specs=[pl.BlockSpec((1, tp, d), lambda s: (s, 0, 0)),
                  pl.BlockSpec(mem_block, mem_map), pl.BlockSpec(mem_block, mem_map)],
        out_specs=pl.BlockSpec((1, tp, d), lambda s: (s, 0, 0)),
        out_shape=jax.ShapeDtypeStruct((n_seq, tp, d), F32),
        compiler_params=_params(("parallel",), blocks, temp_bytes=2 * _nbytes(mem_block, F32)),
        name="memattn_sample",
    )(q, mem_k, mem_v)


A_HALO_ROW = V7X_SUBLANES


def _a_prep_kernel(x_ref, wqkvz_ref, wab_ref, cw_ref, alog_ref, dtb_ref, cs_ref,
                   q_ref, k_ref, v_ref, z_ref, gb_ref, cso_ref, xp_ref, *, tm, n_t, t_valid_last, n_heads):
    t = pl.program_id(1)
    width = cw_ref.shape[0]
    halo = width - 1
    r0 = A_HALO_ROW
    d = q_ref.shape[1]
    dk = d // n_heads

    @pl.when(t == 0)
    def _():
        xp_ref[r0 - halo:r0, :] = cs_ref[0]

    @pl.when(t > 0)
    def _():
        xp_ref[r0 - halo:r0, :] = xp_ref[r0 + tm - halo:r0 + tm, :]

    x = x_ref[...].astype(BF16)
    for seg, o_ref in enumerate((q_ref, k_ref, v_ref)):
        cols = slice(seg * d, (seg + 1) * d)
        xp_ref[r0:r0 + tm, cols] = _dot(x, wqkvz_ref[:, cols])
        y = xp_ref[r0 - halo:r0 - halo + tm, cols] * cw_ref[0:1, cols]
        for i in range(1, width):
            y = y + xp_ref[r0 - halo + i:r0 - halo + i + tm, cols] * cw_ref[i:i + 1, cols]
        y = _silu(y)
        if seg < 2:
            parts = []
            for h in range(n_heads):
                yh = y[:, h * dk:(h + 1) * dk]
                yh = yh * lax.rsqrt(jnp.sum(yh * yh, -1, keepdims=True) + NORM_EPS)
                parts.append(yh * (dk ** -0.5) if seg == 0 else yh)
            y = jnp.concatenate(parts, axis=1)
        o_ref[...] = y
    z_ref[...] = _dot(x, wqkvz_ref[:, 3 * d:4 * d])
    ab = _dot(x, wab_ref[...])
    g = -jnp.exp(alog_ref[...]) * _softplus(ab + dtb_ref[...])
    lane = lax.broadcasted_iota(jnp.int32, ab.shape, 1)
    gb_ref[...] = jnp.where(lane < n_heads, g, jax.nn.sigmoid(ab))

    @pl.when(t == n_t - 1)
    def _():
        cso_ref[0] = xp_ref[r0 + t_valid_last - halo:r0 + t_valid_last, :]


def a_prep(xf, wqkvz, wab, conv_w, alog, dtb, conv_state, n_seq, tm, t_valid_last):
    m, d = xf.shape
    n_t = m // n_seq // tm
    halo = conv_w.shape[0] - 1
    cq = conv_w.shape[1]
    row = lambda s, t: (s * n_t + t, 0)
    const = lambda s, t: (0, 0)
    seq3 = lambda s, t: (s, 0, 0)
    blocks = (_nbytes((tm, d), F32) + _nbytes(wqkvz.shape, BF16) + _nbytes(wab.shape, BF16)
              + 4 * _nbytes((tm, d), F32) + _nbytes((tm, V7X_LANES), F32) + 2 * _nbytes((V7X_SUBLANES, cq), F32))
    scratch = _nbytes((A_HALO_ROW + tm, cq), F32)
    outs = pl.pallas_call(
        functools.partial(_a_prep_kernel, tm=tm, n_t=n_t, t_valid_last=t_valid_last, n_heads=A_HEADS),
        grid=(n_seq, n_t),
        in_specs=[pl.BlockSpec((tm, d), row), pl.BlockSpec(wqkvz.shape, const), pl.BlockSpec(wab.shape, const),
                  pl.BlockSpec(conv_w.shape, const), pl.BlockSpec(alog.shape, const), pl.BlockSpec(dtb.shape, const),
                  pl.BlockSpec((1, halo, cq), seq3)],
        out_specs=[pl.BlockSpec((tm, d), row)] * 4 + [pl.BlockSpec((tm, V7X_LANES), row),
                                                      pl.BlockSpec((1, halo, cq), seq3)],
        out_shape=[jax.ShapeDtypeStruct((m, d), F32)] * 4 + [jax.ShapeDtypeStruct((m, V7X_LANES), F32),
                                                             jax.ShapeDtypeStruct((n_seq, halo, cq), F32)],
        scratch_shapes=[pltpu.VMEM((A_HALO_ROW + tm, cq), F32)],
        compiler_params=_params(("parallel", "arbitrary"), blocks, scratch, temp_bytes=6 * _nbytes((tm, d), F32)),
        name="a_prep",
    )(xf, wqkvz, wab, conv_w, alog, dtb, conv_state)
    return outs


def _gdn_kernel(q_ref, k_ref, v_ref, z_ref, gb_ref, s0_ref, nw_ref, o_ref, s_ref, *, t_valid, n_heads):
    c = pl.program_id(1)
    n_sq, ln = q_ref.shape[:2]
    dk = q_ref.shape[2] // n_heads
    pairs = [(sq, h) for sq in range(n_sq) for h in range(n_heads)]

    @pl.when(c == 0)
    def _():
        s_ref[...] = s0_ref[...]

    row = lax.broadcasted_iota(jnp.int32, (ln, ln), 0)
    col = lax.broadcasted_iota(jnp.int32, (ln, ln), 1)
    tri_incl = row >= col
    strict = row > col
    eye = (row == col).astype(F32)
    row_valid = None
    if t_valid < ln:
        row_valid = lax.broadcasted_iota(jnp.int32, (ln, 1), 0) < t_valid
    gb, g_cum, g_cum_t, g_last = [], [], [], []
    for sq in range(n_sq):
        gb_s = gb_ref[sq] if row_valid is None else jnp.where(row_valid, gb_ref[sq], 0.0)
        g_s = _dot(tri_incl.astype(F32), gb_s, precision=HIGHEST)
        gb.append(gb_s)
        g_cum.append(g_s)
        g_cum_t.append(g_s.T)
        g_last.append(g_s[ln - 1:ln, :])

    def heads(ref):
        return jnp.stack([ref[sq, :, h * dk:(h + 1) * dk] for sq, h in pairs])

    def head_cols(xs, first):
        return jnp.stack([xs[sq][:, first + h:first + h + 1] for sq, h in pairs])

    beta = head_cols(gb, n_heads)
    g_col = head_cols(g_cum, 0)
    e_cum = jnp.exp(g_col)
    e_rem = jnp.exp(head_cols(g_last, 0) - g_col)
    e_last = jnp.exp(head_cols(g_last, 0))
    g_row = jnp.stack([g_cum_t[sq][h:h + 1, :] for sq, h in pairs])
    decay = jnp.where(tri_incl, jnp.exp(g_col - g_row), 0.0)
    q3 = heads(q_ref)
    k3 = heads(k_ref)
    v3 = heads(v_ref)
    if row_valid is not None:
        k3 = jnp.where(row_valid, k3, 0.0)
    k3_b = k3.astype(BF16)
    kb = k3 * beta
    m_mat = jnp.where(strict, _bmm_nt(kb.astype(BF16), k3_b) * decay, 0.0)
    neg = -m_mat
    inv = eye + neg
    pw = neg
    for _ in range(int(math.log2(ln)) - 1):
        pw_b = pw.astype(BF16)
        pw = _bmm(pw_b, pw_b)
        inv = inv + _bmm(inv.astype(BF16), pw.astype(BF16))
    rhs = jnp.concatenate([v3 * beta, kb * e_cum], axis=2)
    sol = _bmm(inv.astype(BF16), rhs.astype(BF16))
    u = sol[:, :, :dk]
    w = sol[:, :, dk:]
    s_all = s_ref[...].reshape((n_sq * n_heads,) + s_ref.shape[2:])
    s_b = s_all.astype(BF16)
    v_new = u - _bmm(w.astype(BF16), s_b)
    v_new_b = v_new.astype(BF16)
    intra = _bmm_nt(q3.astype(BF16), k3_b) * decay
    o = _bmm((q3 * e_cum).astype(BF16), s_b) + _bmm(intra.astype(BF16), v_new_b)
    kd_t = jnp.swapaxes(k3 * e_rem, 1, 2).astype(BF16)
    s_ref[...] = (s_all * e_last + _bmm(kd_t, v_new_b)).reshape(s_ref.shape)
    o = o * lax.rsqrt(jnp.mean(o * o, -1, keepdims=True) + NORM_EPS) * nw_ref[...] * _silu(heads(z_ref))
    for i, (sq, h) in enumerate(pairs):
        o_ref[sq, :, h * dk:(h + 1) * dk] = o[i].astype(o_ref.dtype)


def gdn(q, k, v, z, gb, s0, norm_w, n_seq, t_valid):
    m, d = q.shape
    ln = A_CHUNK
    t_len = m // n_seq
    n_c = t_len // ln
    n_sq = GDN_SEQS_PER_STEP if n_seq % GDN_SEQS_PER_STEP == 0 else 1
    seq_chunk = lambda s, c: (s, c, 0)
    const = lambda s, c: (0, 0)
    seq4 = lambda s, c: (s, 0, 0, 0)
    sblock = (n_sq,) + s0.shape[1:]
    blocks = (4 * _nbytes((n_sq, ln, d), F32) + _nbytes((n_sq, ln, V7X_LANES), F32) + 2 * _nbytes(sblock, F32)
              + _nbytes((n_sq, ln, d), BF16))
    r3 = lambda a: a.reshape(n_seq, t_len, a.shape[1])
    o, s_new = pl.pallas_call(
        functools.partial(_gdn_kernel, t_valid=t_valid, n_heads=A_HEADS),
        grid=(n_seq // n_sq, n_c),
        in_specs=[pl.BlockSpec((n_sq, ln, d), seq_chunk)] * 4
                 + [pl.BlockSpec((n_sq, ln, V7X_LANES), seq_chunk), pl.BlockSpec(sblock, seq4),
                    pl.BlockSpec(norm_w.shape, const)],
        out_specs=[pl.BlockSpec((n_sq, ln, d), seq_chunk), pl.BlockSpec(sblock, seq4)],
        out_shape=[jax.ShapeDtypeStruct((n_seq, t_len, d), BF16), jax.ShapeDtypeStruct(s0.shape, F32)],
        compiler_params=_params(("parallel", "arbitrary"), blocks, temp_bytes=16 << 20),
        name="gdn",
    )(r3(q), r3(k), r3(v), r3(z), r3(gb), s0, norm_w)
    return o.reshape(m, d), s_new


def _b_mixer_kernel(res_ref, win_ref, cw_ref, wout_ref, cs_ref, g_ref, b_ref,
                    of_ref, ob_ref, cso_ref, xp_ref, *, alpha, tm, n_t, t_valid_last):
    t = pl.program_id(1)
    width = cw_ref.shape[0]
    halo = width - 1
    r0 = A_HALO_ROW
    d = res_ref.shape[1]

    @pl.when(t == 0)
    def _():
        xp_ref[r0 - halo:r0, :] = cs_ref[0]

    @pl.when(t > 0)
    def _():
        xp_ref[r0 - halo:r0, :] = xp_ref[r0 + tm - halo:r0 + tm, :]

    x = res_ref[...].astype(BF16)
    gate_c = _dot(x, win_ref[:, d:2 * d])
    hid = _dot(x, win_ref[:, 2 * d:3 * d])
    xp_ref[r0:r0 + tm, :] = gate_c * hid
    y = xp_ref[r0 - halo:r0 - halo + tm, :] * cw_ref[0:1, :]
    for i in range(1, width):
        y = y + xp_ref[r0 - halo + i:r0 - halo + i + tm, :] * cw_ref[i:i + 1, :]
    gate_b = _dot(x, win_ref[:, 0:d])
    out = _dot((gate_b * y).astype(BF16), wout_ref[...])
    v = _layer_norm(alpha * res_ref[...] + out, g_ref[...], b_ref[...])
    of_ref[...] = v
    ob_ref[...] = v.astype(BF16)

    @pl.when(t == n_t - 1)
    def _():
        cso_ref[0] = xp_ref[r0 + t_valid_last - halo:r0 + t_valid_last, :]


def b_mixer(xf, w_in, conv_w, w_out, conv_state, g, b, alpha, n_seq, tm, t_valid_last):
    m, d = xf.shape
    n_t = m // n_seq // tm
    halo = conv_w.shape[0] - 1
    row = lambda s, t: (s * n_t + t, 0)
    const = lambda s, t: (0, 0)
    seq3 = lambda s, t: (s, 0, 0)
    blocks = (_nbytes((tm, d), BF16) + 2 * _nbytes((tm, d), F32) + _nbytes(w_in.shape, BF16)
              + _nbytes(w_out.shape, BF16) + 2 * _nbytes((V7X_SUBLANES, d), F32))
    scratch = _nbytes((A_HALO_ROW + tm, d), F32)
    return pl.pallas_call(
        functools.partial(_b_mixer_kernel, alpha=alpha, tm=tm, n_t=n_t, t_valid_last=t_valid_last),
        grid=(n_seq, n_t),
        in_specs=[pl.BlockSpec((tm, d), row), pl.BlockSpec(w_in.shape, const),
                  pl.BlockSpec(conv_w.shape, const), pl.BlockSpec(w_out.shape, const),
                  pl.BlockSpec((1, halo, d), seq3), pl.BlockSpec((1, d), const), pl.BlockSpec((1, d), const)],
        out_specs=[pl.BlockSpec((tm, d), row), pl.BlockSpec((tm, d), row), pl.BlockSpec((1, halo, d), seq3)],
        out_shape=[jax.ShapeDtypeStruct((m, d), F32), jax.ShapeDtypeStruct((m, d), BF16),
                   jax.ShapeDtypeStruct((n_seq, halo, d), F32)],
        scratch_shapes=[pltpu.VMEM((A_HALO_ROW + tm, d), F32)],
        compiler_params=_params(("parallel", "arbitrary"), blocks, scratch, temp_bytes=6 * _nbytes((tm, d), F32)),
        name="b_mixer",
    )(xf, w_in, conv_w, w_out, conv_state, g, b)


def _suffix_matrix(n):
    r = lax.broadcasted_iota(jnp.int32, (n, n), 0)
    c = lax.broadcasted_iota(jnp.int32, (n, n), 1)
    return (r > c).astype(BF16)


def _suffix_sum(l, u):
    l_hi = l.astype(BF16)
    l_lo = (l - l_hi.astype(F32)).astype(BF16)
    return _dot(l_hi, u) + _dot(l_lo, u)


def _sb_prompt_kernel(q_ref, k_ref, v_ref, u_ref, o_ref, *, tq, dh):
    i = pl.program_id(2)
    lanes = q_ref.shape[1]
    lane = lax.broadcasted_iota(jnp.int32, (1, lanes), 1)
    n_sub = lanes // dh
    q = q_ref[...]
    qs = [jnp.where(lane // dh == hh, q, jnp.zeros_like(q)) for hh in range(n_sub)]
    r = lax.broadcasted_iota(jnp.int32, (tq, tq), 0)
    c = lax.broadcasted_iota(jnp.int32, (tq, tq), 1)
    causal = c < r
    u = u_ref[...]

    def visit(tiles, state, first_is_diagonal):
        accs, carries = state
        chains = [(t, hh) for t in range(len(tiles)) for hh in range(n_sub)]
        starts = [pl.multiple_of(j * tq, tq) for j in tiles]
        ks = [k_ref[pl.ds(st, tq), :] for st in starts]
        vs = [v_ref[pl.ds(st, tq), :] for st in starts]
        s = {c: _dot_nt(qs[c[1]], ks[c[0]]) for c in chains}
        lsn = {c: _log_sigmoid_neg(s[c]) for c in chains}
        masked = lambda c: first_is_diagonal and c[0] == 0
        l = {c: jnp.where(causal, lsn[c], 0.0) if masked(c) else lsn[c] for c in chains}
        carry_in = {}
        carries = list(carries)
        for t, hh in chains:
            carry_in[(t, hh)] = carries[hh]
            carries[hh] = carries[hh] + jnp.sum(l[(t, hh)], -1, keepdims=True)
        suffix = {c: _suffix_sum(l[c], u) for c in chains}
        w = {c: jnp.exp(s[c] + lsn[c] + suffix[c] + carry_in[c]) for c in chains}
        w = {c: jnp.where(causal, w[c], 0.0) if masked(c) else w[c] for c in chains}
        accs = list(accs)
        for t, hh in chains:
            accs[hh] = accs[hh] + _dot(w[(t, hh)].astype(BF16), vs[t])
        return tuple(accs), tuple(carries)

    init = (tuple(jnp.zeros((tq, lanes), F32) for _ in range(n_sub)),
            tuple(jnp.zeros((tq, 1), F32) for _ in range(n_sub)))
    group = SB_TILES_PER_ITER
    extra = lax.rem(i, group)
    state = lax.switch(extra, [functools.partial(lambda n, st: visit([i - t for t in range(n + 1)], st, True), n)
                               for n in range(group)], init)
    first = i - 1 - extra
    state = lax.fori_loop(0, lax.div(i, group),
                          lambda t, st: visit([first - group * t - g for g in range(group)], st, False), state)
    accs = state[0]
    out = accs[0]
    for hh in range(1, n_sub):
        out = jnp.where(lane // dh == hh, accs[hh], out)
    o_ref[...] = out.astype(o_ref.dtype)


def sb_prompt(q, k, v, n_seq, tq):
    m, d = q.shape
    t_len = m // n_seq
    n_q = t_len // tq
    dh = d // C_HEADS
    lanes = V7X_LANES
    u = jnp.tril(jnp.ones((tq, tq), BF16), k=-1)
    blocks = 2 * _nbytes((tq, lanes), BF16) + 2 * _nbytes((t_len, lanes), BF16) + _nbytes((tq, tq), BF16)
    return pl.pallas_call(
        functools.partial(_sb_prompt_kernel, tq=tq, dh=dh),
        grid=(n_seq, d // lanes, n_q),
        in_specs=[pl.BlockSpec((tq, lanes), lambda s, p, i: (s * n_q + i, p)),
                  pl.BlockSpec((t_len, lanes), lambda s, p, i: (s, p)),
                  pl.BlockSpec((t_len, lanes), lambda s, p, i: (s, p)),
                  pl.BlockSpec((tq, tq), lambda s, p, i: (0, 0))],
        out_specs=pl.BlockSpec((tq, lanes), lambda s, p, i: (s * n_q + i, p)),
        out_shape=jax.ShapeDtypeStruct((m, d), BF16),
        compiler_params=_params(("parallel", "parallel", "arbitrary"), blocks, temp_bytes=16 * _nbytes((tq, tq), F32)),
        name="sb_prompt",
    )(q, k, v, u)


def _sb_paged_kernel(pt_ref, q_ref, kn_ref, vn_ref, u_ref, *refs, n_new, n_heads, pages_per_step):
    del pt_ref
    k_refs = refs[:pages_per_step]
    v_refs = refs[pages_per_step:2 * pages_per_step]
    o_ref = refs[2 * pages_per_step]
    q3_ref, acc_ref, carry_ref = refs[2 * pages_per_step + 1:]
    j = pl.program_id(1)
    tp, d = q_ref.shape[1:]
    dh = d // n_heads
    page = k_refs[0].shape[3]
    n_rows = n_heads * tp

    def heads(x):
        return jnp.stack([x[:, h * dh:(h + 1) * dh] for h in range(n_heads)])

    @pl.when(j == 0)
    def _():
        q3 = heads(q_ref[0])
        q3_ref[...] = q3.astype(BF16)
        kn3 = heads(kn_ref[0])
        vn3 = heads(vn_ref[0])
        q_pos = lax.broadcasted_iota(jnp.int32, (1, tp, 1), 1)
        carry = jnp.zeros((n_heads, tp, 1), F32)
        acc = jnp.zeros((n_heads, tp, dh), F32)
        for t_key in reversed(range(n_new)):
            valid = t_key < q_pos
            s = jnp.sum(q3 * kn3[:, t_key:t_key + 1, :], -1, keepdims=True)
            lsn = _log_sigmoid_neg(s)
            w = jnp.where(valid, jnp.exp(s + lsn + carry), 0.0)
            acc = acc + w * vn3[:, t_key:t_key + 1, :]
            carry = carry + jnp.where(valid, lsn, 0.0)
        acc_ref[...] = acc
        carry_ref[...] = jnp.broadcast_to(carry.reshape(n_rows, 1), carry_ref.shape)

    q3_b = q3_ref[...]
    u = u_ref[...]
    per_block = u.shape[0] // page
    blocks = [range(b0, b0 + per_block) for b0 in reversed(range(0, pages_per_step, per_block))]
    cat = lambda refs, blk: jnp.concatenate([refs[p][0] for p in blk], axis=2).astype(BF16)
    s = [_bmm(q3_b, cat(k_refs, blk)).reshape(n_rows, u.shape[0]) for blk in blocks]
    l = [_log_sigmoid_neg(sb) for sb in s]
    carry = carry_ref[:, 0:1]
    carry_in = []
    for lb in l:
        carry_in.append(carry)
        carry = carry + jnp.sum(lb, -1, keepdims=True)
    suffix = [_suffix_sum(lb, u) for lb in l]
    w = [jnp.exp(sb + lb + fb + cb).reshape(n_heads, tp, u.shape[0]).astype(BF16)
         for sb, lb, fb, cb in zip(s, l, suffix, carry_in)]
    acc = acc_ref[...]
    for wb, blk in zip(w, blocks):
        acc = acc + _bmm_nt(wb, cat(v_refs, blk))
    acc_ref[...] = acc
    carry_ref[...] = jnp.broadcast_to(carry, carry_ref.shape)

    @pl.when(j == pl.num_programs(1) - 1)
    def _():
        res = acc_ref[...]
        out = jnp.concatenate([res[h] for h in range(n_heads)], axis=1)
        is_new = lax.broadcasted_iota(jnp.int32, (tp, 1), 0) < n_new
        o_ref[0] = jnp.where(is_new, out, 0.0).astype(o_ref.dtype)


def sb_paged(q, k_new, v_new, cache_k, cache_v, page_table, n_new, pages_per_step=8, pages_per_block=4):
    n_seq, tp, d = q.shape
    n_pages = page_table.shape[1]
    _, n_heads, dh, page = cache_k.shape
    assert n_pages % pages_per_step == 0 and pages_per_step % pages_per_block == 0
    n_steps = n_pages // pages_per_step
    n_rows = n_heads * tp
    blk_keys = pages_per_block * page
    u = jnp.tril(jnp.ones((blk_keys, blk_keys), BF16), k=-1)
    pt_flat = page_table.reshape(-1)

    def page_map(p):
        return lambda s, j, pt: (pt[s * n_pages + (n_steps - 1 - j) * pages_per_step + p], 0, 0, 0)

    seq3 = lambda s, j, pt: (s, 0, 0)
    page_block = (1, n_heads, dh, page)
    blocks = 4 * _nbytes((tp, d), F32) + _nbytes(u.shape, BF16) + 2 * pages_per_step * _nbytes(page_block, F32)
    scratch = (_nbytes((n_heads, 2 * tp, V7X_LANES), BF16) + _nbytes((n_heads, tp, V7X_LANES), F32)
               + _nbytes((n_rows, V7X_LANES), F32))
    grid_spec = pltpu.PrefetchScalarGridSpec(
        num_scalar_prefetch=1,
        grid=(n_seq, n_steps),
        in_specs=[pl.BlockSpec((1, tp, d), seq3), pl.BlockSpec((1, tp, d), seq3), pl.BlockSpec((1, tp, d), seq3),
                  pl.BlockSpec(u.shape, lambda s, j, pt: (0, 0))]
                 + [pl.BlockSpec(page_block, page_map(p)) for p in range(pages_per_step)]
                 + [pl.BlockSpec(page_block, page_map(p)) for p in range(pages_per_step)],
        out_specs=pl.BlockSpec((1, tp, d), seq3),
        scratch_shapes=[pltpu.VMEM((n_heads, tp, dh), BF16), pltpu.VMEM((n_heads, tp, dh), F32),
                        pltpu.VMEM((n_rows, V7X_LANES), F32)],
    )
    return pl.pallas_call(
        functools.partial(_sb_paged_kernel, n_new=n_new, n_heads=n_heads, pages_per_step=pages_per_step),
        grid_spec=grid_spec,
        out_shape=jax.ShapeDtypeStruct((n_seq, tp, d), F32),
        compiler_params=_params(("parallel", "arbitrary"), blocks, scratch, temp_bytes=8 << 20),
        name="sb_paged",
    )(pt_flat, q, k_new, v_new, u, *([cache_k] * pages_per_step), *([cache_v] * pages_per_step))


def _trunk(xf, n_seq, t_pad, t_valid, a_conv, a_ssm, b_conv, mem, past, weights, tm_tok, tm_seq, sb_tq):
    (a_wqkvz, a_wab, a_conv_w, a_alog, a_dtb, a_norm_w, a_w_out, b_w_in, b_conv_w, b_w_out,
     c_w_qkv, c_b_qkv, c_w_out, m_w_q, m_w_o, f_w1, f_w2, ln_g, ln_b, alpha) = weights
    m, d = xf.shape
    depth = ln_g.shape[0]
    xb = xf.astype(BF16)
    zero_bias = jnp.zeros((1, d), F32)
    new_a_conv, new_a_ssm, new_b_conv, new_c_k, new_c_v = [], [], [], [], []
    t_valid_last = t_valid - (t_pad - tm_seq)
    for i in range(depth):
        kind, j = i % 3, i // 3
        g = lambda n: ln_g[i, n][None]
        b = lambda n: ln_b[i, n][None]
        if kind == 0:
            q, k, v, z, gb, cso = a_prep(xf, a_wqkvz[j], a_wab[j], a_conv_w[j], a_alog[j], a_dtb[j], a_conv[j],
                                        n_seq, tm_seq, t_valid_last)
            new_a_conv.append(cso)
            if t_pad % A_CHUNK:
                pad = lambda a: jnp.pad(a.reshape(n_seq, t_pad, -1), ((0, 0), (0, A_CHUNK - t_pad), (0, 0))
                                        ).reshape(n_seq * A_CHUNK, -1)
                o, s_new = gdn(pad(q), pad(k), pad(v), pad(z), pad(gb), a_ssm[j], a_norm_w[j], n_seq, t_valid)
                o = o.reshape(n_seq, A_CHUNK, d)[:, :t_pad].reshape(m, d)
            else:
                o, s_new = gdn(q, k, v, z, gb, a_ssm[j], a_norm_w[j], n_seq, A_CHUNK)
            new_a_ssm.append(s_new)
            xf, xb = mm_ln(o, a_w_out[j], xf, g(0), b(0), alpha, tm_tok)
        elif kind == 1:
            xf, xb, cso = b_mixer(xf, b_w_in[j], b_conv_w[j], b_w_out[j], b_conv[j], g(0), b(0), alpha,
                                  n_seq, tm_seq, t_valid_last)
            new_b_conv.append(cso)
        else:
            dh = d // C_HEADS
            if past is None:
                qs, kb16, vb16, kf, vf = qkv_prompt(xb, c_w_qkv[j], c_b_qkv[j], dh ** -0.5, n_seq, tm_seq)
                o = sb_prompt(qs, kb16, vb16, n_seq, sb_tq)
            else:
                qs, kf, vf = mm_split(xb, c_w_qkv[j], c_b_qkv[j], [[F32], [F32], [F32]], [dh ** -0.5, 1.0, 1.0],
                                      tm_tok)
                r3 = lambda a: a.reshape(n_seq, t_pad, d)
                o = sb_paged(r3(qs), r3(kf), r3(vf), past[0][j], past[1][j], past[2], t_valid)
                o = o.reshape(m, d).astype(BF16)
            new_c_k.append(kf)
            new_c_v.append(vf)
            xf, xb = mm_ln(o, c_w_out[j], xf, g(0), b(0), alpha, tm_tok)
        if past is None:
            mk, mv = mem[i]
            xf, xb = memattn_prompt(xb, xf, mk, mv, m_w_q[i], m_w_o[i], g(1), b(1), alpha, n_seq, tm_seq)
        else:
            (qm,) = mm_split(xb, m_w_q[i], zero_bias, [[F32]], [(d // M_HEADS) ** -0.5], tm_tok)
            o = memattn_sample_core(qm.reshape(n_seq, t_pad, d), mem[0], mem[1], i).reshape(m, d).astype(BF16)
            xf, xb = mm_ln(o, m_w_o[i], xf, g(1), b(1), alpha, tm_tok)
        xf, xb = mlp_ln(xb, f_w1[i], f_w2[i], xf, g(2), b(2), alpha, tm_tok)
    return xf, new_a_conv, new_a_ssm, new_b_conv, new_c_k, new_c_v


def kernel(x_prompt, x_sample, state_a_conv, state_a_ssm, state_b_conv, cache_c_k, cache_c_v, cache_mem_k, cache_mem_v, page_table, mem_prompt, a_w_in, a_conv_w, a_a_log, a_dt_bias, a_norm_w, a_w_out, b_w_in, b_conv_w, b_w_out, c_w_qkv, c_b_qkv, c_w_out, m_w_q, m_w_kv, m_w_o, f_w1, f_w2, ln_g, ln_b):
    bp, t_p, d = x_prompt.shape
    bs, t_s, _ = x_sample.shape
    depth = ln_g.shape[0]
    alpha = (2.0 * depth) ** 0.25
    n_a, n_b, n_c = a_w_in.shape[0], b_w_in.shape[0], c_w_qkv.shape[0]
    a_qkv = a_conv_w.shape[2]
    n_mem = mem_prompt.shape[1]
    bf = lambda a: a.astype(BF16)

    a_wqkvz = bf(a_w_in[:, :, :a_qkv + d])
    ab_cols = a_w_in.shape[2] - (a_qkv + d)
    a_wab = bf(jnp.pad(a_w_in[:, :, a_qkv + d:], ((0, 0), (0, 0), (0, V7X_LANES - ab_cols))))
    lane_pad = lambda a: jnp.pad(a, ((0, 0), (0, V7X_LANES - a.shape[1])))[:, None, :]
    weights = (a_wqkvz, a_wab, a_conv_w, lane_pad(a_a_log), lane_pad(a_dt_bias), a_norm_w[:, None, :], bf(a_w_out),
               bf(b_w_in), b_conv_w, bf(b_w_out), bf(c_w_qkv), c_b_qkv.reshape(n_c, 1, 3 * d), bf(c_w_out),
               bf(m_w_q), bf(m_w_o), bf(f_w1), bf(f_w2), ln_g, ln_b, alpha)

    dm = d // M_HEADS
    n_tiles = dm // V7X_LANES
    mem_rows = lambda a, n: jnp.transpose(a.reshape(depth, n, n_mem, M_HEADS, n_tiles, V7X_LANES),
                                          (0, 1, 2, 4, 3, 5)).reshape(depth, n, n_mem * n_tiles * M_HEADS, V7X_LANES)
    from_rows = lambda a, n: jnp.transpose(a.reshape(depth, n, n_mem, n_tiles, M_HEADS, V7X_LANES),
                                           (0, 1, 2, 4, 3, 5)).reshape(depth, n, n_mem, M_HEADS, dm)
    mem_kb, mem_vb, mem_kr, mem_vr = memkv_proj(bf(mem_prompt.reshape(bp * n_mem, d)), bf(m_w_kv), bp)
    p_mem = [(mem_kb[i], mem_vb[i]) for i in range(depth)]
    p_mem_k = from_rows(mem_kr, bp)
    p_mem_v = from_rows(mem_vr, bp)
    yp, pa_conv, pa_ssm, pb_conv, pc_k, pc_v = _trunk(
        x_prompt.reshape(bp * t_p, d), bp, t_p, t_p,
        jnp.zeros((n_a, bp) + state_a_conv.shape[2:], F32), jnp.zeros((n_a, bp) + state_a_ssm.shape[2:], F32),
        jnp.zeros((n_b, bp) + state_b_conv.shape[2:], F32), p_mem, None, weights,
        tm_tok=PROMPT_TILE, tm_seq=min(PROMPT_TILE, t_p), sb_tq=min(PROMPT_TILE, t_p))

    tp = SAMPLE_T_PAD
    xs = jnp.pad(x_sample, ((0, 0), (0, tp - t_s), (0, 0))).reshape(bs * tp, d)
    s_mem = (mem_rows(cache_mem_k, bs), mem_rows(cache_mem_v, bs))
    past = (jnp.transpose(cache_c_k, (0, 1, 3, 4, 2)), jnp.transpose(cache_c_v, (0, 1, 3, 4, 2)), page_table)
    ys, sa_conv, sa_ssm, sb_conv, sc_k, sc_v = _trunk(
        xs, bs, tp, t_s, state_a_conv, state_a_ssm, state_b_conv, s_mem, past, weights,
        tm_tok=bs * tp, tm_seq=tp, sb_tq=None)

    dh = d // C_HEADS
    unpad = lambda a: a.reshape(bs, tp, -1)[:, :t_s]
    from_t = lambda a: jnp.transpose(jnp.stack(a).reshape(n_c, bp, C_HEADS, dh, t_p), (0, 1, 4, 2, 3))
    return (yp.reshape(bp, t_p, d), unpad(ys),
            jnp.stack(pa_conv), jnp.stack(pa_ssm), jnp.stack(pb_conv),
            from_t(pc_k), from_t(pc_v),
            p_mem_k, p_mem_v,
            jnp.stack(sa_conv), jnp.stack(sa_ssm), jnp.stack(sb_conv),
            jnp.stack([unpad(a) for a in sc_k]).reshape(n_c, bs, t_s, C_HEADS, dh),
            jnp.stack([unpad(a) for a in sc_v]).reshape(n_c, bs, t_s, C_HEADS, dh))
```

```python
import functools
import math

import jax
import jax.numpy as jnp
from jax import lax
from jax.experimental import pallas as pl
from jax.experimental.pallas import tpu as pltpu

F32 = jnp.float32
BF16 = jnp.bfloat16
HIGHEST = lax.Precision.HIGHEST

V7X_LANES = 128
V7X_SUBLANES = 8
V7X_VMEM_BYTES = 64 * 1024 * 1024
VMEM_CAP_BYTES = V7X_VMEM_BYTES - 8 * 1024 * 1024

LN_EPS = 1e-5
NORM_EPS = 1e-6
A_CHUNK = 64
A_HEADS = 8
A_DK = 128
C_HEADS = 16
M_HEADS = 4
SAMPLE_T_PAD = V7X_SUBLANES
PROMPT_TILE = 256
SB_TILES_PER_ITER = 4
GDN_SEQS_PER_STEP = 4

def _vmem_limit(block_bytes, scratch_bytes=0, temp_bytes=0):
    est = 2 * block_bytes + scratch_bytes + temp_bytes + (4 << 20)
    return int(min(max(est, 16 << 20), VMEM_CAP_BYTES))


def _nbytes(shape, dtype):
    return math.prod(shape) * jnp.dtype(dtype).itemsize


def _params(sem, block_bytes, scratch_bytes=0, temp_bytes=0):
    return pltpu.CompilerParams(dimension_semantics=sem,
                                vmem_limit_bytes=_vmem_limit(block_bytes, scratch_bytes, temp_bytes))


def _layer_norm(v, g, b):
    mu = jnp.mean(v, -1, keepdims=True)
    c = v - mu
    var = jnp.mean(c * c, -1, keepdims=True)
    return c * lax.rsqrt(var + LN_EPS) * g + b


def _softplus(x):
    return jnp.maximum(x, 0.0) + jnp.log1p(jnp.exp(-jnp.abs(x)))


def _log_sigmoid_neg(x):
    return jnp.minimum(-x, 0.0) - jnp.log(1.0 + jnp.exp(-jnp.abs(x)))


def _silu(x):
    return x * jax.nn.sigmoid(x)


def _dot(a, b, **kw):
    return jnp.dot(a, b, preferred_element_type=F32, **kw)


def _dot_nt(a, b):
    return lax.dot_general(a, b, (((1,), (1,)), ((), ())), preferred_element_type=F32)


def _bmm(a, b):
    return lax.dot_general(a, b, (((2,), (1,)), ((0,), (0,))), preferred_element_type=F32)


def _bmm_nt(a, b):
    return lax.dot_general(a, b, (((2,), (2,)), ((0,), (0,))), preferred_element_type=F32)


def _mm_split_kernel(x_ref, w_ref, b_ref, *o_refs, d, scales, n_copies):
    x = x_ref[...]
    k = 0
    for j, n_out in enumerate(n_copies):
        cols = slice(j * d, (j + 1) * d)
        y = _dot(x, w_ref[:, cols]) + b_ref[:, cols]
        if scales[j] != 1.0:
            y = y * scales[j]
        for _ in range(n_out):
            o_refs[k][...] = y.astype(o_refs[k].dtype)
            k += 1


def mm_split(x, w, bias, out_dtypes, scales, tm):
    m, kdim = x.shape
    ns = len(out_dtypes)
    d = w.shape[1] // ns
    flat = [dt for group in out_dtypes for dt in group]
    blocks = (_nbytes((tm, kdim), x.dtype) + _nbytes(w.shape, w.dtype) + _nbytes(bias.shape, F32)
              + sum(_nbytes((tm, d), dt) for dt in flat))
    return pl.pallas_call(
        functools.partial(_mm_split_kernel, d=d, scales=tuple(scales), n_copies=tuple(len(g) for g in out_dtypes)),
        grid=(m // tm,),
        in_specs=[pl.BlockSpec((tm, kdim), lambda i: (i, 0)),
                  pl.BlockSpec(w.shape, lambda i: (0, 0)),
                  pl.BlockSpec(bias.shape, lambda i: (0, 0))],
        out_specs=[pl.BlockSpec((tm, d), lambda i: (i, 0)) for _ in flat],
        out_shape=[jax.ShapeDtypeStruct((m, d), dt) for dt in flat],
        compiler_params=_params(("parallel",), blocks, temp_bytes=2 * _nbytes((tm, d), F32)),
        name="mm_split",
    )(x, w, bias)


def _memkv_kernel(x_ref, w_ref, kb_ref, vb_ref, kr_ref, vr_ref, *, n_heads):
    x = x_ref[...]
    d = kb_ref.shape[2]
    dh = d // n_heads
    n_tiles = dh // V7X_LANES
    group = n_heads * n_tiles
    n_mem = x.shape[0]
    for j, (b_ref, r_ref) in enumerate(((kb_ref, kr_ref), (vb_ref, vr_ref))):
        y = _dot(x, w_ref[0, :, j * d:(j + 1) * d])
        b_ref[0] = y.astype(b_ref.dtype)
        for h in range(n_heads):
            for lt in range(n_tiles):
                lo = h * dh + lt * V7X_LANES
                r_ref[0, 0, pl.ds(lt * n_heads + h, n_mem, stride=group), :] = y[:, lo:lo + V7X_LANES]


def memkv_proj(x, w_kv, n_seq):
    m, d = x.shape
    depth = w_kv.shape[0]
    n_mem = m // n_seq
    rows = n_mem * d // V7X_LANES
    blocks = (_nbytes((n_mem, d), BF16) + _nbytes((d, 2 * d), BF16) + 2 * _nbytes((n_mem, d), BF16)
              + 2 * _nbytes((n_mem, d), F32))
    return pl.pallas_call(
        functools.partial(_memkv_kernel, n_heads=M_HEADS),
        grid=(depth, n_seq),
        in_specs=[pl.BlockSpec((n_mem, d), lambda l, s: (s, 0)),
                  pl.BlockSpec((1, d, 2 * d), lambda l, s: (l, 0, 0))],
        out_specs=[pl.BlockSpec((1, n_mem, d), lambda l, s: (l, s, 0))] * 2
                  + [pl.BlockSpec((1, 1, rows, V7X_LANES), lambda l, s: (l, s, 0, 0))] * 2,
        out_shape=[jax.ShapeDtypeStruct((depth, m, d), BF16)] * 2
                  + [jax.ShapeDtypeStruct((depth, n_seq, rows, V7X_LANES), F32)] * 2,
        compiler_params=_params(("parallel", "parallel"), blocks, temp_bytes=4 * _nbytes((n_mem, d), F32)),
        name="memkv_proj",
    )(x, w_kv)


def _qkv_prompt_kernel(x_ref, w_ref, b_ref, q_ref, k_ref, v_ref, kt_ref, vt_ref, *, q_scale):
    x = x_ref[...]
    d = q_ref.shape[1]
    q = _dot(x, w_ref[:, 0:d]) + b_ref[:, 0:d]
    q_ref[...] = (q * q_scale).astype(q_ref.dtype)
    for j, (o_ref, t_ref) in enumerate(((k_ref, kt_ref), (v_ref, vt_ref)), start=1):
        y = _dot(x, w_ref[:, j * d:(j + 1) * d]) + b_ref[:, j * d:(j + 1) * d]
        o_ref[...] = y.astype(o_ref.dtype)
        t_ref[0] = y.T


def qkv_prompt(x, w, bias, q_scale, n_seq, tm):
    m, kdim = x.shape
    d = w.shape[1] // 3
    t_len = m // n_seq
    n_t = t_len // tm
    row = lambda s, t: (s * n_t + t, 0)
    const = lambda s, t: (0, 0)
    blocks = (_nbytes((tm, kdim), BF16) + _nbytes(w.shape, BF16) + 3 * _nbytes((tm, d), BF16)
              + 2 * _nbytes((d, tm), F32))
    return pl.pallas_call(
        functools.partial(_qkv_prompt_kernel, q_scale=q_scale),
        grid=(n_seq, n_t),
        in_specs=[pl.BlockSpec((tm, kdim), row), pl.BlockSpec(w.shape, const), pl.BlockSpec(bias.shape, const)],
        out_specs=[pl.BlockSpec((tm, d), row)] * 3 + [pl.BlockSpec((1, d, tm), lambda s, t: (s, 0, t))] * 2,
        out_shape=[jax.ShapeDtypeStruct((m, d), BF16)] * 3 + [jax.ShapeDtypeStruct((n_seq, d, t_len), F32)] * 2,
        compiler_params=_params(("parallel", "parallel"), blocks, temp_bytes=4 * _nbytes((tm, d), F32)),
        name="qkv_prompt",
    )(x, w, bias)


def _mm_ln_kernel(x_ref, w_ref, res_ref, g_ref, b_ref, of_ref, ob_ref, *, alpha):
    y = _dot(x_ref[...], w_ref[...])
    v = _layer_norm(alpha * res_ref[...] + y, g_ref[...], b_ref[...])
    of_ref[...] = v
    ob_ref[...] = v.astype(BF16)


def mm_ln(x, w, res, g, b, alpha, tm):
    m, kdim = x.shape
    d = w.shape[1]
    blocks = (_nbytes((tm, kdim), BF16) + _nbytes(w.shape, BF16) + 2 * _nbytes((tm, d), F32)
              + _nbytes((tm, d), BF16))
    row = lambda i: (i, 0)
    const = lambda i: (0, 0)
    return pl.pallas_call(
        functools.partial(_mm_ln_kernel, alpha=alpha),
        grid=(m // tm,),
        in_specs=[pl.BlockSpec((tm, kdim), row), pl.BlockSpec(w.shape, const), pl.BlockSpec((tm, d), row),
                  pl.BlockSpec((1, d), const), pl.BlockSpec((1, d), const)],
        out_specs=[pl.BlockSpec((tm, d), row), pl.BlockSpec((tm, d), row)],
        out_shape=[jax.ShapeDtypeStruct((m, d), F32), jax.ShapeDtypeStruct((m, d), BF16)],
        compiler_params=_params(("parallel",), blocks, temp_bytes=3 * _nbytes((tm, d), F32)),
        name="mm_ln",
    )(x, w, res, g, b)


def _mlp_kernel(x_ref, w1_ref, w2_ref, res_ref, g_ref, b_ref, of_ref, ob_ref, *, alpha, tf):
    x = x_ref[...]
    acc = None
    for c in range(w1_ref.shape[1] // tf):
        cols = slice(c * tf, (c + 1) * tf)
        h = jnp.square(jnp.maximum(_dot(x, w1_ref[:, cols]), 0.0)).astype(BF16)
        part = _dot(h, w2_ref[cols, :])
        acc = part if acc is None else acc + part
    v = _layer_norm(alpha * res_ref[...] + acc, g_ref[...], b_ref[...])
    of_ref[...] = v
    ob_ref[...] = v.astype(BF16)


def mlp_ln(x, w1, w2, res, g, b, alpha, tm, tf=1024):
    m, d = x.shape
    blocks = (_nbytes((tm, d), BF16) + _nbytes(w1.shape, BF16) + _nbytes(w2.shape, BF16)
              + 2 * _nbytes((tm, d), F32) + _nbytes((tm, d), BF16))
    row = lambda i: (i, 0)
    const = lambda i: (0, 0)
    return pl.pallas_call(
        functools.partial(_mlp_kernel, alpha=alpha, tf=tf),
        grid=(m // tm,),
        in_specs=[pl.BlockSpec((tm, d), row), pl.BlockSpec(w1.shape, const), pl.BlockSpec(w2.shape, const),
                  pl.BlockSpec((tm, d), row), pl.BlockSpec((1, d), const), pl.BlockSpec((1, d), const)],
        out_specs=[pl.BlockSpec((tm, d), row), pl.BlockSpec((tm, d), row)],
        out_shape=[jax.ShapeDtypeStruct((m, d), F32), jax.ShapeDtypeStruct((m, d), BF16)],
        compiler_params=_params(("parallel",), blocks, temp_bytes=3 * _nbytes((tm, tf), F32) + 2 * _nbytes((tm, d), F32)),
        name="mlp_ln",
    )(x, w1, w2, res, g, b)


def _memattn_kernel(x_ref, res_ref, mk_ref, mv_ref, wq_ref, wo_ref, g_ref, b_ref, of_ref, ob_ref, *, alpha, n_heads):
    d = x_ref.shape[1]
    dh = d // n_heads
    q = (_dot(x_ref[...], wq_ref[...]) * (dh ** -0.5)).astype(BF16)
    cols = [slice(h * dh, (h + 1) * dh) for h in range(n_heads)]
    s = [_dot_nt(q[:, c], mk_ref[:, c]) for c in cols]
    e = [jnp.exp(sh - jnp.max(sh, -1, keepdims=True)) for sh in s]
    p = [(eh / jnp.sum(eh, -1, keepdims=True)).astype(BF16) for eh in e]
    o = jnp.concatenate([_dot(ph, mv_ref[:, c]).astype(BF16) for ph, c in zip(p, cols)], axis=1)
    v = _layer_norm(alpha * res_ref[...] + _dot(o, wo_ref[...]), g_ref[...], b_ref[...])
    of_ref[...] = v
    ob_ref[...] = v.astype(BF16)


def memattn_prompt(xb, xf, mk, mv, wq, wo, g, b, alpha, n_seq, tm):
    m, d = xb.shape
    n_t = m // n_seq // tm
    n_mem = mk.shape[0] // n_seq
    blocks = (_nbytes((tm, d), BF16) * 2 + _nbytes((tm, d), F32) * 2 + 2 * _nbytes((n_mem, d), BF16)
              + 2 * _nbytes(wq.shape, BF16))
    row = lambda s, t: (s * n_t + t, 0)
    seq = lambda s, t: (s, 0)
    const = lambda s, t: (0, 0)
    return pl.pallas_call(
        functools.partial(_memattn_kernel, alpha=alpha, n_heads=M_HEADS),
        grid=(n_seq, n_t),
        in_specs=[pl.BlockSpec((tm, d), row), pl.BlockSpec((tm, d), row),
                  pl.BlockSpec((n_mem, d), seq), pl.BlockSpec((n_mem, d), seq),
                  pl.BlockSpec(wq.shape, const), pl.BlockSpec(wo.shape, const),
                  pl.BlockSpec((1, d), const), pl.BlockSpec((1, d), const)],
        out_specs=[pl.BlockSpec((tm, d), row), pl.BlockSpec((tm, d), row)],
        out_shape=[jax.ShapeDtypeStruct((m, d), F32), jax.ShapeDtypeStruct((m, d), BF16)],
        compiler_params=_params(("parallel", "parallel"), blocks, temp_bytes=6 * _nbytes((tm, d), F32)),
        name="memattn_prompt",
    )(xb, xf, mk, mv, wq, wo, g, b)


def _memattn_sample_kernel(q_ref, mk_ref, mv_ref, o_ref, *, n_heads):
    q = q_ref[0].astype(BF16)
    d = q.shape[1]
    dh = d // n_heads
    n_tiles = dh // V7X_LANES
    group = n_heads * n_tiles
    n_mem = mk_ref.shape[2] // group
    slab = lambda ref, h, lt: ref[0, 0, pl.ds(lt * n_heads + h, n_mem, stride=group), :].astype(BF16)
    lanes = lambda h, lt: slice(h * dh + lt * V7X_LANES, h * dh + (lt + 1) * V7X_LANES)
    s = [sum(_dot_nt(q[:, lanes(h, lt)], slab(mk_ref, h, lt)) for lt in range(n_tiles)) for h in range(n_heads)]
    e = [jnp.exp(sh - jnp.max(sh, -1, keepdims=True)) for sh in s]
    p = [(eh / jnp.sum(eh, -1, keepdims=True)).astype(BF16) for eh in e]
    o = [_dot(p[h], slab(mv_ref, h, lt)) for h in range(n_heads) for lt in range(n_tiles)]
    o_ref[0] = jnp.concatenate(o, axis=1).astype(o_ref.dtype)


def memattn_sample_core(q, mem_k, mem_v, layer):
    n_seq, tp, d = q.shape
    mem_block = (1, 1) + mem_k.shape[2:]
    blocks = 2 * _nbytes((tp, d), F32) + 2 * _nbytes(mem_block, F32)
    mem_map = lambda s: (layer, s, 0, 0)
    return pl.pallas_call(
        functools.partial(_memattn_sample_kernel, n_heads=M_HEADS),
        grid=(n_seq,),
        in_specs=[pl.BlockSpec((1, tp, d), lambda s: (s, 0, 0)),
                  pl.BlockSpec(mem_block, mem_map), pl.BlockSpec(mem_block, mem_map)],
        out_specs=pl.BlockSpec((1, tp, d), lambda s: (s, 0, 0)),
        out_shape=jax.ShapeDtypeStruct((n_seq, tp, d), F32),
        compiler_params=_params(("parallel",), blocks, temp_bytes=2 * _nbytes(mem_block, F32)),
        name="memattn_sample",
    )(q, mem_k, mem_v)


A_HALO_ROW = V7X_SUBLANES


def _a_prep_kernel(x_ref, wqkvz_ref, wab_ref, cw_ref, alog_ref, dtb_ref, cs_ref,
                   q_ref, k_ref, v_ref, z_ref, gb_ref, cso_ref, xp_ref, *, tm, n_t, t_valid_last, n_heads):
    t = pl.program_id(1)
    width = cw_ref.shape[0]
    halo = width - 1
    r0 = A_HALO_ROW
    d = q_ref.shape[1]
    dk = d // n_heads

    @pl.when(t == 0)
    def _():
        xp_ref[r0 - halo:r0, :] = cs_ref[0]

    @pl.when(t > 0)
    def _():
        xp_ref[r0 - halo:r0, :] = xp_ref[r0 + tm - halo:r0 + tm, :]

    x = x_ref[...].astype(BF16)
    for seg, o_ref in enumerate((q_ref, k_ref, v_ref)):
        cols = slice(seg * d, (seg + 1) * d)
        xp_ref[r0:r0 + tm, cols] = _dot(x, wqkvz_ref[:, cols])
        y = xp_ref[r0 - halo:r0 - halo + tm, cols] * cw_ref[0:1, cols]
        for i in range(1, width):
            y = y + xp_ref[r0 - halo + i:r0 - halo + i + tm, cols] * cw_ref[i:i + 1, cols]
        y = _silu(y)
        if seg < 2:
            parts = []
            for h in range(n_heads):
                yh = y[:, h * dk:(h + 1) * dk]
                yh = yh * lax.rsqrt(jnp.sum(yh * yh, -1, keepdims=True) + NORM_EPS)
                parts.append(yh * (dk ** -0.5) if seg == 0 else yh)
            y = jnp.concatenate(parts, axis=1)
        o_ref[...] = y
    z_ref[...] = _dot(x, wqkvz_ref[:, 3 * d:4 * d])
    ab = _dot(x, wab_ref[...])
    g = -jnp.exp(alog_ref[...]) * _softplus(ab + dtb_ref[...])
    lane = lax.broadcasted_iota(jnp.int32, ab.shape, 1)
    gb_ref[...] = jnp.where(lane < n_heads, g, jax.nn.sigmoid(ab))

    @pl.when(t == n_t - 1)
    def _():
        cso_ref[0] = xp_ref[r0 + t_valid_last - halo:r0 + t_valid_last, :]


def a_prep(xf, wqkvz, wab, conv_w, alog, dtb, conv_state, n_seq, tm, t_valid_last):
    m, d = xf.shape
    n_t = m // n_seq // tm
    halo = conv_w.shape[0] - 1
    cq = conv_w.shape[1]
    row = lambda s, t: (s * n_t + t, 0)
    const = lambda s, t: (0, 0)
    seq3 = lambda s, t: (s, 0, 0)
    blocks = (_nbytes((tm, d), F32) + _nbytes(wqkvz.shape, BF16) + _nbytes(wab.shape, BF16)
              + 4 * _nbytes((tm, d), F32) + _nbytes((tm, V7X_LANES), F32) + 2 * _nbytes((V7X_SUBLANES, cq), F32))
    scratch = _nbytes((A_HALO_ROW + tm, cq), F32)
    outs = pl.pallas_call(
        functools.partial(_a_prep_kernel, tm=tm, n_t=n_t, t_valid_last=t_valid_last, n_heads=A_HEADS),
        grid=(n_seq, n_t),
        in_specs=[pl.BlockSpec((tm, d), row), pl.BlockSpec(wqkvz.shape, const), pl.BlockSpec(wab.shape, const),
                  pl.BlockSpec(conv_w.shape, const), pl.BlockSpec(alog.shape, const), pl.BlockSpec(dtb.shape, const),
                  pl.BlockSpec((1, halo, cq), seq3)],
        out_specs=[pl.BlockSpec((tm, d), row)] * 4 + [pl.BlockSpec((tm, V7X_LANES), row),
                                                      pl.BlockSpec((1, halo, cq), seq3)],
        out_shape=[jax.ShapeDtypeStruct((m, d), F32)] * 4 + [jax.ShapeDtypeStruct((m, V7X_LANES), F32),
                                                             jax.ShapeDtypeStruct((n_seq, halo, cq), F32)],
        scratch_shapes=[pltpu.VMEM((A_HALO_ROW + tm, cq), F32)],
        compiler_params=_params(("parallel", "arbitrary"), blocks, scratch, temp_bytes=6 * _nbytes((tm, d), F32)),
        name="a_prep",
    )(xf, wqkvz, wab, conv_w, alog, dtb, conv_state)
    return outs


def _gdn_kernel(q_ref, k_ref, v_ref, z_ref, gb_ref, s0_ref, nw_ref, o_ref, s_ref, *, t_valid, n_heads):
    c = pl.program_id(1)
    n_sq, ln = q_ref.shape[:2]
    dk = q_ref.shape[2] // n_heads
    pairs = [(sq, h) for sq in range(n_sq) for h in range(n_heads)]

    @pl.when(c == 0)
    def _():
        s_ref[...] = s0_ref[...]

    row = lax.broadcasted_iota(jnp.int32, (ln, ln), 0)
    col = lax.broadcasted_iota(jnp.int32, (ln, ln), 1)
    tri_incl = row >= col
    strict = row > col
    eye = (row == col).astype(F32)
    row_valid = None
    if t_valid < ln:
        row_valid = lax.broadcasted_iota(jnp.int32, (ln, 1), 0) < t_valid
    gb, g_cum, g_cum_t, g_last = [], [], [], []
    for sq in range(n_sq):
        gb_s = gb_ref[sq] if row_valid is None else jnp.where(row_valid, gb_ref[sq], 0.0)
        g_s = _dot(tri_incl.astype(F32), gb_s, precision=HIGHEST)
        gb.append(gb_s)
        g_cum.append(g_s)
        g_cum_t.append(g_s.T)
        g_last.append(g_s[ln - 1:ln, :])

    def heads(ref):
        return jnp.stack([ref[sq, :, h * dk:(h + 1) * dk] for sq, h in pairs])

    def head_cols(xs, first):
        return jnp.stack([xs[sq][:, first + h:first + h + 1] for sq, h in pairs])

    beta = head_cols(gb, n_heads)
    g_col = head_cols(g_cum, 0)
    e_cum = jnp.exp(g_col)
    e_rem = jnp.exp(head_cols(g_last, 0) - g_col)
    e_last = jnp.exp(head_cols(g_last, 0))
    g_row = jnp.stack([g_cum_t[sq][h:h + 1, :] for sq, h in pairs])
    decay = jnp.where(tri_incl, jnp.exp(g_col - g_row), 0.0)
    q3 = heads(q_ref)
    k3 = heads(k_ref)
    v3 = heads(v_ref)
    if row_valid is not None:
        k3 = jnp.where(row_valid, k3, 0.0)
    k3_b = k3.astype(BF16)
    kb = k3 * beta
    m_mat = jnp.where(strict, _bmm_nt(kb.astype(BF16), k3_b) * decay, 0.0)
    neg = -m_mat
    inv = eye + neg
    pw = neg
    for _ in range(int(math.log2(ln)) - 1):
        pw_b = pw.astype(BF16)
        pw = _bmm(pw_b, pw_b)
        inv = inv + _bmm(inv.astype(BF16), pw.astype(BF16))
    rhs = jnp.concatenate([v3 * beta, kb * e_cum], axis=2)
    sol = _bmm(inv.astype(BF16), rhs.astype(BF16))
    u = sol[:, :, :dk]
    w = sol[:, :, dk:]
    s_all = s_ref[...].reshape((n_sq * n_heads,) + s_ref.shape[2:])
    s_b = s_all.astype(BF16)
    v_new = u - _bmm(w.astype(BF16), s_b)
    v_new_b = v_new.astype(BF16)
    intra = _bmm_nt(q3.astype(BF16), k3_b) * decay
    o = _bmm((q3 * e_cum).astype(BF16), s_b) + _bmm(intra.astype(BF16), v_new_b)
    kd_t = jnp.swapaxes(k3 * e_rem, 1, 2).astype(BF16)
    s_ref[...] = (s_all * e_last + _bmm(kd_t, v_new_b)).reshape(s_ref.shape)
    o = o * lax.rsqrt(jnp.mean(o * o, -1, keepdims=True) + NORM_EPS) * nw_ref[...] * _silu(heads(z_ref))
    for i, (sq, h) in enumerate(pairs):
        o_ref[sq, :, h * dk:(h + 1) * dk] = o[i].astype(o_ref.dtype)


def gdn(q, k, v, z, gb, s0, norm_w, n_seq, t_valid):
    m, d = q.shape
    ln = A_CHUNK
    t_len = m // n_seq
    n_c = t_len // ln
    n_sq = GDN_SEQS_PER_STEP if n_seq % GDN_SEQS_PER_STEP == 0 else 1
    seq_chunk = lambda s, c: (s, c, 0)
    const = lambda s, c: (0, 0)
    seq4 = lambda s, c: (s, 0, 0, 0)
    sblock = (n_sq,) + s0.shape[1:]
    blocks = (4 * _nbytes((n_sq, ln, d), F32) + _nbytes((n_sq, ln, V7X_LANES), F32) + 2 * _nbytes(sblock, F32)
              + _nbytes((n_sq, ln, d), BF16))
    r3 = lambda a: a.reshape(n_seq, t_len, a.shape[1])
    o, s_new = pl.pallas_call(
        functools.partial(_gdn_kernel, t_valid=t_valid, n_heads=A_HEADS),
        grid=(n_seq // n_sq, n_c),
        in_specs=[pl.BlockSpec((n_sq, ln, d), seq_chunk)] * 4
                 + [pl.BlockSpec((n_sq, ln, V7X_LANES), seq_chunk), pl.BlockSpec(sblock, seq4),
                    pl.BlockSpec(norm_w.shape, const)],
        out_specs=[pl.BlockSpec((n_sq, ln, d), seq_chunk), pl.BlockSpec(sblock, seq4)],
        out_shape=[jax.ShapeDtypeStruct((n_seq, t_len, d), BF16), jax.ShapeDtypeStruct(s0.shape, F32)],
        compiler_params=_params(("parallel", "arbitrary"), blocks, temp_bytes=16 << 20),
        name="gdn",
    )(r3(q), r3(k), r3(v), r3(z), r3(gb), s0, norm_w)
    return o.reshape(m, d), s_new


def _b_mixer_kernel(res_ref, win_ref, cw_ref, wout_ref, cs_ref, g_ref, b_ref,
                    of_ref, ob_ref, cso_ref, xp_ref, *, alpha, tm, n_t, t_valid_last):
    t = pl.program_id(1)
    width = cw_ref.shape[0]
    halo = width - 1
    r0 = A_HALO_ROW
    d = res_ref.shape[1]

    @pl.when(t == 0)
    def _():
        xp_ref[r0 - halo:r0, :] = cs_ref[0]

    @pl.when(t > 0)
    def _():
        xp_ref[r0 - halo:r0, :] = xp_ref[r0 + tm - halo:r0 + tm, :]

    x = res_ref[...].astype(BF16)
    gate_c = _dot(x, win_ref[:, d:2 * d])
    hid = _dot(x, win_ref[:, 2 * d:3 * d])
    xp_ref[r0:r0 + tm, :] = gate_c * hid
    y = xp_ref[r0 - halo:r0 - halo + tm, :] * cw_ref[0:1, :]
    for i in range(1, width):
        y = y + xp_ref[r0 - halo + i:r0 - halo + i + tm, :] * cw_ref[i:i + 1, :]
    gate_b = _dot(x, win_ref[:, 0:d])
    out = _dot((gate_b * y).astype(BF16), wout_ref[...])
    v = _layer_norm(alpha * res_ref[...] + out, g_ref[...], b_ref[...])
    of_ref[...] = v
    ob_ref[...] = v.astype(BF16)

    @pl.when(t == n_t - 1)
    def _():
        cso_ref[0] = xp_ref[r0 + t_valid_last - halo:r0 + t_valid_last, :]


def b_mixer(xf, w_in, conv_w, w_out, conv_state, g, b, alpha, n_seq, tm, t_valid_last):
    m, d = xf.shape
    n_t = m // n_seq // tm
    halo = conv_w.shape[0] - 1
    row = lambda s, t: (s * n_t + t, 0)
    const = lambda s, t: (0, 0)
    seq3 = lambda s, t: (s, 0, 0)
    blocks = (_nbytes((tm, d), BF16) + 2 * _nbytes((tm, d), F32) + _nbytes(w_in.shape, BF16)
              + _nbytes(w_out.shape, BF16) + 2 * _nbytes((V7X_SUBLANES, d), F32))
    scratch = _nbytes((A_HALO_ROW + tm, d), F32)
    return pl.pallas_call(
        functools.partial(_b_mixer_kernel, alpha=alpha, tm=tm, n_t=n_t, t_valid_last=t_valid_last),
        grid=(n_seq, n_t),
        in_specs=[pl.BlockSpec((tm, d), row), pl.BlockSpec(w_in.shape, const),
                  pl.BlockSpec(conv_w.shape, const), pl.BlockSpec(w_out.shape, const),
                  pl.BlockSpec((1, halo, d), seq3), pl.BlockSpec((1, d), const), pl.BlockSpec((1, d), const)],
        out_specs=[pl.BlockSpec((tm, d), row), pl.BlockSpec((tm, d), row), pl.BlockSpec((1, halo, d), seq3)],
        out_shape=[jax.ShapeDtypeStruct((m, d), F32), jax.ShapeDtypeStruct((m, d), BF16),
                   jax.ShapeDtypeStruct((n_seq, halo, d), F32)],
        scratch_shapes=[pltpu.VMEM((A_HALO_ROW + tm, d), F32)],
        compiler_params=_params(("parallel", "arbitrary"), blocks, scratch, temp_bytes=6 * _nbytes((tm, d), F32)),
        name="b_mixer",
    )(xf, w_in, conv_w, w_out, conv_state, g, b)


def _suffix_matrix(n):
    r = lax.broadcasted_iota(jnp.int32, (n, n), 0)
    c = lax.broadcasted_iota(jnp.int32, (n, n), 1)
    return (r > c).astype(BF16)


def _suffix_sum(l, u):
    l_hi = l.astype(BF16)
    l_lo = (l - l_hi.astype(F32)).astype(BF16)
    return _dot(l_hi, u) + _dot(l_lo, u)


def _sb_prompt_kernel(q_ref, k_ref, v_ref, u_ref, o_ref, *, tq, dh):
    i = pl.program_id(2)
    lanes = q_ref.shape[1]
    lane = lax.broadcasted_iota(jnp.int32, (1, lanes), 1)
    n_sub = lanes // dh
    q = q_ref[...]
    qs = [jnp.where(lane // dh == hh, q, jnp.zeros_like(q)) for hh in range(n_sub)]
    r = lax.broadcasted_iota(jnp.int32, (tq, tq), 0)
    c = lax.broadcasted_iota(jnp.int32, (tq, tq), 1)
    causal = c < r
    u = u_ref[...]

    def visit(tiles, state, first_is_diagonal):
        accs, carries = state
        chains = [(t, hh) for t in range(len(tiles)) for hh in range(n_sub)]
        starts = [pl.multiple_of(j * tq, tq) for j in tiles]
        ks = [k_ref[pl.ds(st, tq), :] for st in starts]
        vs = [v_ref[pl.ds(st, tq), :] for st in starts]
        s = {c: _dot_nt(qs[c[1]], ks[c[0]]) for c in chains}
        lsn = {c: _log_sigmoid_neg(s[c]) for c in chains}
        masked = lambda c: first_is_diagonal and c[0] == 0
        l = {c: jnp.where(causal, lsn[c], 0.0) if masked(c) else lsn[c] for c in chains}
        carry_in = {}
        carries = list(carries)
        for t, hh in chains:
            carry_in[(t, hh)] = carries[hh]
            carries[hh] = carries[hh] + jnp.sum(l[(t, hh)], -1, keepdims=True)
        suffix = {c: _suffix_sum(l[c], u) for c in chains}
        w = {c: jnp.exp(s[c] + lsn[c] + suffix[c] + carry_in[c]) for c in chains}
        w = {c: jnp.where(causal, w[c], 0.0) if masked(c) else w[c] for c in chains}
        accs = list(accs)
        for t, hh in chains:
            accs[hh] = accs[hh] + _dot(w[(t, hh)].astype(BF16), vs[t])
        return tuple(accs), tuple(carries)

    init = (tuple(jnp.zeros((tq, lanes), F32) for _ in range(n_sub)),
            tuple(jnp.zeros((tq, 1), F32) for _ in range(n_sub)))
    group = SB_TILES_PER_ITER
    extra = lax.rem(i, group)
    state = lax.switch(extra, [functools.partial(lambda n, st: visit([i - t for t in range(n + 1)], st, True), n)
                               for n in range(group)], init)
    first = i - 1 - extra
    state = lax.fori_loop(0, lax.div(i, group),
                          lambda t, st: visit([first - group * t - g for g in range(group)], st, False), state)
    accs = state[0]
    out = accs[0]
    for hh in range(1, n_sub):
        out = jnp.where(lane // dh == hh, accs[hh], out)
    o_ref[...] = out.astype(o_ref.dtype)


def sb_prompt(q, k, v, n_seq, tq):
    m, d = q.shape
    t_len = m // n_seq
    n_q = t_len // tq
    dh = d // C_HEADS
    lanes = V7X_LANES
    u = jnp.tril(jnp.ones((tq, tq), BF16), k=-1)
    blocks = 2 * _nbytes((tq, lanes), BF16) + 2 * _nbytes((t_len, lanes), BF16) + _nbytes((tq, tq), BF16)
    return pl.pallas_call(
        functools.partial(_sb_prompt_kernel, tq=tq, dh=dh),
        grid=(n_seq, d // lanes, n_q),
        in_specs=[pl.BlockSpec((tq, lanes), lambda s, p, i: (s * n_q + i, p)),
                  pl.BlockSpec((t_len, lanes), lambda s, p, i: (s, p)),
                  pl.BlockSpec((t_len, lanes), lambda s, p, i: (s, p)),
                  pl.BlockSpec((tq, tq), lambda s, p, i: (0, 0))],
        out_specs=pl.BlockSpec((tq, lanes), lambda s, p, i: (s * n_q + i, p)),
        out_shape=jax.ShapeDtypeStruct((m, d), BF16),
        compiler_params=_params(("parallel", "parallel", "arbitrary"), blocks, temp_bytes=16 * _nbytes((tq, tq), F32)),
        name="sb_prompt",
    )(q, k, v, u)


def _sb_paged_kernel(pt_ref, q_ref, kn_ref, vn_ref, u_ref, *refs, n_new, n_heads, pages_per_step):
    del pt_ref
    k_refs = refs[:pages_per_step]
    v_refs = refs[pages_per_step:2 * pages_per_step]
    o_ref = refs[2 * pages_per_step]
    q3_ref, acc_ref, carry_ref = refs[2 * pages_per_step + 1:]
    j = pl.program_id(1)
    tp, d = q_ref.shape[1:]
    dh = d // n_heads
    page = k_refs[0].shape[3]
    n_rows = n_heads * tp

    def heads(x):
        return jnp.stack([x[:, h * dh:(h + 1) * dh] for h in range(n_heads)])

    @pl.when(j == 0)
    def _():
        q3 = heads(q_ref[0])
        q3_ref[...] = q3.astype(BF16)
        kn3 = heads(kn_ref[0])
        vn3 = heads(vn_ref[0])
        q_pos = lax.broadcasted_iota(jnp.int32, (1, tp, 1), 1)
        carry = jnp.zeros((n_heads, tp, 1), F32)
        acc = jnp.zeros((n_heads, tp, dh), F32)
        for t_key in reversed(range(n_new)):
            valid = t_key < q_pos
            s = jnp.sum(q3 * kn3[:, t_key:t_key + 1, :], -1, keepdims=True)
            lsn = _log_sigmoid_neg(s)
            w = jnp.where(valid, jnp.exp(s + lsn + carry), 0.0)
            acc = acc + w * vn3[:, t_key:t_key + 1, :]
            carry = carry + jnp.where(valid, lsn, 0.0)
        acc_ref[...] = acc
        carry_ref[...] = jnp.broadcast_to(carry.reshape(n_rows, 1), carry_ref.shape)

    q3_b = q3_ref[...]
    u = u_ref[...]
    per_block = u.shape[0] // page
    blocks = [range(b0, b0 + per_block) for b0 in reversed(range(0, pages_per_step, per_block))]
    cat = lambda refs, blk: jnp.concatenate([refs[p][0] for p in blk], axis=2).astype(BF16)
    s = [_bmm(q3_b, cat(k_refs, blk)).reshape(n_rows, u.shape[0]) for blk in blocks]
    l = [_log_sigmoid_neg(sb) for sb in s]
    carry = carry_ref[:, 0:1]
    carry_in = []
    for lb in l:
        carry_in.append(carry)
        carry = carry + jnp.sum(lb, -1, keepdims=True)
    suffix = [_suffix_sum(lb, u) for lb in l]
    w = [jnp.exp(sb + lb + fb + cb).reshape(n_heads, tp, u.shape[0]).astype(BF16)
         for sb, lb, fb, cb in zip(s, l, suffix, carry_in)]
    acc = acc_ref[...]
    for wb, blk in zip(w, blocks):
        acc = acc + _bmm_nt(wb, cat(v_refs, blk))
    acc_ref[...] = acc
    carry_ref[...] = jnp.broadcast_to(carry, carry_ref.shape)

    @pl.when(j == pl.num_programs(1) - 1)
    def _():
        res = acc_ref[...]
        out = jnp.concatenate([res[h] for h in range(n_heads)], axis=1)
        is_new = lax.broadcasted_iota(jnp.int32, (tp, 1), 0) < n_new
        o_ref[0] = jnp.where(is_new, out, 0.0).astype(o_ref.dtype)


def sb_paged(q, k_new, v_new, cache_k, cache_v, page_table, n_new, pages_per_step=16, pages_per_block=4):
    n_seq, tp, d = q.shape
    n_pages = page_table.shape[1]
    _, n_heads, dh, page = cache_k.shape
    assert n_pages % pages_per_step == 0 and pages_per_step % pages_per_block == 0
    n_steps = n_pages // pages_per_step
    n_rows = n_heads * tp
    blk_keys = pages_per_block * page
    u = jnp.tril(jnp.ones((blk_keys, blk_keys), BF16), k=-1)
    pt_flat = page_table.reshape(-1)

    def page_map(p):
        return lambda s, j, pt: (pt[s * n_pages + (n_steps - 1 - j) * pages_per_step + p], 0, 0, 0)

    seq3 = lambda s, j, pt: (s, 0, 0)
    page_block = (1, n_heads, dh, page)
    blocks = 4 * _nbytes((tp, d), F32) + _nbytes(u.shape, BF16) + 2 * pages_per_step * _nbytes(page_block, F32)
    scratch = (_nbytes((n_heads, 2 * tp, V7X_LANES), BF16) + _nbytes((n_heads, tp, V7X_LANES), F32)
               + _nbytes((n_rows, V7X_LANES), F32))
    grid_spec = pltpu.PrefetchScalarGridSpec(
        num_scalar_prefetch=1,
        grid=(n_seq, n_steps),
        in_specs=[pl.BlockSpec((1, tp, d), seq3), pl.BlockSpec((1, tp, d), seq3), pl.BlockSpec((1, tp, d), seq3),
                  pl.BlockSpec(u.shape, lambda s, j, pt: (0, 0))]
                 + [pl.BlockSpec(page_block, page_map(p)) for p in range(pages_per_step)]
                 + [pl.BlockSpec(page_block, page_map(p)) for p in range(pages_per_step)],
        out_specs=pl.BlockSpec((1, tp, d), seq3),
        scratch_shapes=[pltpu.VMEM((n_heads, tp, dh), BF16), pltpu.VMEM((n_heads, tp, dh), F32),
                        pltpu.VMEM((n_rows, V7X_LANES), F32)],
    )
    return pl.pallas_call(
        functools.partial(_sb_paged_kernel, n_new=n_new, n_heads=n_heads, pages_per_step=pages_per_step),
        grid_spec=grid_spec,
        out_shape=jax.ShapeDtypeStruct((n_seq, tp, d), F32),
        compiler_params=_params(("parallel", "arbitrary"), blocks, scratch, temp_bytes=8 << 20),
        name="sb_paged",
    )(pt_flat, q, k_new, v_new, u, *([cache_k] * pages_per_step), *([cache_v] * pages_per_step))


def _trunk(xf, n_seq, t_pad, t_valid, a_conv, a_ssm, b_conv, mem, past, weights, tm_tok, tm_seq, sb_tq):
    (a_wqkvz, a_wab, a_conv_w, a_alog, a_dtb, a_norm_w, a_w_out, b_w_in, b_conv_w, b_w_out,
     c_w_qkv, c_b_qkv, c_w_out, m_w_q, m_w_o, f_w1, f_w2, ln_g, ln_b, alpha) = weights
    m, d = xf.shape
    depth = ln_g.shape[0]
    xb = xf.astype(BF16)
    zero_bias = jnp.zeros((1, d), F32)
    new_a_conv, new_a_ssm, new_b_conv, new_c_k, new_c_v = [], [], [], [], []
    t_valid_last = t_valid - (t_pad - tm_seq)
    for i in range(depth):
        kind, j = i % 3, i // 3
        g = lambda n: ln_g[i, n][None]
        b = lambda n: ln_b[i, n][None]
        if kind == 0:
            q, k, v, z, gb, cso = a_prep(xf, a_wqkvz[j], a_wab[j], a_conv_w[j], a_alog[j], a_dtb[j], a_conv[j],
                                        n_seq, tm_seq, t_valid_last)
            new_a_conv.append(cso)
            if t_pad % A_CHUNK:
                pad = lambda a: jnp.pad(a.reshape(n_seq, t_pad, -1), ((0, 0), (0, A_CHUNK - t_pad), (0, 0))
                                        ).reshape(n_seq * A_CHUNK, -1)
                o, s_new = gdn(pad(q), pad(k), pad(v), pad(z), pad(gb), a_ssm[j], a_norm_w[j], n_seq, t_valid)
                o = o.reshape(n_seq, A_CHUNK, d)[:, :t_pad].reshape(m, d)
            else:
                o, s_new = gdn(q, k, v, z, gb, a_ssm[j], a_norm_w[j], n_seq, A_CHUNK)
            new_a_ssm.append(s_new)
            xf, xb = mm_ln(o, a_w_out[j], xf, g(0), b(0), alpha, tm_tok)
        elif kind == 1:
            xf, xb, cso = b_mixer(xf, b_w_in[j], b_conv_w[j], b_w_out[j], b_conv[j], g(0), b(0), alpha,
                                  n_seq, tm_seq, t_valid_last)
            new_b_conv.append(cso)
        else:
            dh = d // C_HEADS
            if past is None:
                qs, kb16, vb16, kf, vf = qkv_prompt(xb, c_w_qkv[j], c_b_qkv[j], dh ** -0.5, n_seq, tm_seq)
                o = sb_prompt(qs, kb16, vb16, n_seq, sb_tq)
            else:
                qs, kf, vf = mm_split(xb, c_w_qkv[j], c_b_qkv[j], [[F32], [F32], [F32]], [dh ** -0.5, 1.0, 1.0],
                                      tm_tok)
                r3 = lambda a: a.reshape(n_seq, t_pad, d)
                o = sb_paged(r3(qs), r3(kf), r3(vf), past[0][j], past[1][j], past[2], t_valid)
                o = o.reshape(m, d).astype(BF16)
            new_c_k.append(kf)
            new_c_v.append(vf)
            xf, xb = mm_ln(o, c_w_out[j], xf, g(0), b(0), alpha, tm_tok)
        if past is None:
            mk, mv = mem[i]
            xf, xb = memattn_prompt(xb, xf, mk, mv, m_w_q[i], m_w_o[i], g(1), b(1), alpha, n_seq, tm_seq)
        else:
            (qm,) = mm_split(xb, m_w_q[i], zero_bias, [[F32]], [(d // M_HEADS) ** -0.5], tm_tok)
            o = memattn_sample_core(qm.reshape(n_seq, t_pad, d), mem[0], mem[1], i).reshape(m, d).astype(BF16)
            xf, xb = mm_ln(o, m_w_o[i], xf, g(1), b(1), alpha, tm_tok)
        xf, xb = mlp_ln(xb, f_w1[i], f_w2[i], xf, g(2), b(2), alpha, tm_tok)
    return xf, new_a_conv, new_a_ssm, new_b_conv, new_c_k, new_c_v


def kernel(x_prompt, x_sample, state_a_conv, state_a_ssm, state_b_conv, cache_c_k, cache_c_v, cache_mem_k, cache_mem_v, page_table, mem_prompt, a_w_in, a_conv_w, a_a_log, a_dt_bias, a_norm_w, a_w_out, b_w_in, b_conv_w, b_w_out, c_w_qkv, c_b_qkv, c_w_out, m_w_q, m_w_kv, m_w_o, f_w1, f_w2, ln_g, ln_b):
    bp, t_p, d = x_prompt.shape
    bs, t_s, _ = x_sample.shape
    depth = ln_g.shape[0]
    alpha = (2.0 * depth) ** 0.25
    n_a, n_b, n_c = a_w_in.shape[0], b_w_in.shape[0], c_w_qkv.shape[0]
    a_qkv = a_conv_w.shape[2]
    n_mem = mem_prompt.shape[1]
    bf = lambda a: a.astype(BF16)

    a_wqkvz = bf(a_w_in[:, :, :a_qkv + d])
    ab_cols = a_w_in.shape[2] - (a_qkv + d)
    a_wab = bf(jnp.pad(a_w_in[:, :, a_qkv + d:], ((0, 0), (0, 0), (0, V7X_LANES - ab_cols))))
    lane_pad = lambda a: jnp.pad(a, ((0, 0), (0, V7X_LANES - a.shape[1])))[:, None, :]
    weights = (a_wqkvz, a_wab, a_conv_w, lane_pad(a_a_log), lane_pad(a_dt_bias), a_norm_w[:, None, :], bf(a_w_out),
               bf(b_w_in), b_conv_w, bf(b_w_out), bf(c_w_qkv), c_b_qkv.reshape(n_c, 1, 3 * d), bf(c_w_out),
               bf(m_w_q), bf(m_w_o), bf(f_w1), bf(f_w2), ln_g, ln_b, alpha)

    dm = d // M_HEADS
    n_tiles = dm // V7X_LANES
    mem_rows = lambda a, n: jnp.transpose(a.reshape(depth, n, n_mem, M_HEADS, n_tiles, V7X_LANES),
                                          (0, 1, 2, 4, 3, 5)).reshape(depth, n, n_mem * n_tiles * M_HEADS, V7X_LANES)
    from_rows = lambda a, n: jnp.transpose(a.reshape(depth, n, n_mem, n_tiles, M_HEADS, V7X_LANES),
                                           (0, 1, 2, 4, 3, 5)).reshape(depth, n, n_mem, M_HEADS, dm)
    mem_kb, mem_vb, mem_kr, mem_vr = memkv_proj(bf(mem_prompt.reshape(bp * n_mem, d)), bf(m_w_kv), bp)
    p_mem = [(mem_kb[i], mem_vb[i]) for i in range(depth)]
    p_mem_k = from_rows(mem_kr, bp)
    p_mem_v = from_rows(mem_vr, bp)
    yp, pa_conv, pa_ssm, pb_conv, pc_k, pc_v = _trunk(
        x_prompt.reshape(bp * t_p, d), bp, t_p, t_p,
        jnp.zeros((n_a, bp) + state_a_conv.shape[2:], F32), jnp.zeros((n_a, bp) + state_a_ssm.shape[2:], F32),
        jnp.zeros((n_b, bp) + state_b_conv.shape[2:], F32), p_mem, None, weights,
        tm_tok=PROMPT_TILE, tm_seq=min(PROMPT_TILE, t_p), sb_tq=min(PROMPT_TILE, t_p))

    tp = SAMPLE_T_PAD
    xs = jnp.pad(x_sample, ((0, 0), (0, tp - t_s), (0, 0))).reshape(bs * tp, d)
    s_mem = (mem_rows(cache_mem_k, bs), mem_rows(cache_mem_v, bs))
    past = (jnp.transpose(cache_c_k, (0, 1, 3, 4, 2)), jnp.transpose(cache_c_v, (0, 1, 3, 4, 2)), page_table)
    ys, sa_conv, sa_ssm, sb_conv, sc_k, sc_v = _trunk(
        xs, bs, tp, t_s, state_a_conv, state_a_ssm, state_b_conv, s_mem, past, weights,
        tm_tok=bs * tp, tm_seq=tp, sb_tq=None)

    dh = d // C_HEADS
    unpad = lambda a: a.reshape(bs, tp, -1)[:, :t_s]
    from_t = lambda a: jnp.transpose(jnp.stack(a).reshape(n_c, bp, C_HEADS, dh, t_p), (0, 1, 4, 2, 3))
    return (yp.reshape(bp, t_p, d), unpad(ys),
            jnp.stack(pa_conv), jnp.stack(pa_ssm), jnp.stack(pb_conv),
            from_t(pc_k), from_t(pc_v),
            p_mem_k, p_mem_v,
            jnp.stack(sa_conv), jnp.stack(sa_ssm), jnp.stack(sb_conv),
            jnp.stack([unpad(a) for a in sc_k]).reshape(n_c, bs, t_s, C_HEADS, dh),
            jnp.stack([unpad(a) for a in sc_v]).reshape(n_c, bs, t_s, C_HEADS, dh))
```
